```python
import numpy as np
import jax
import jax.numpy as jnp
from jax import lax

D_MODEL = 1024
BATCH = 16
SEQ = 2048
DEPTH = 2

HEAD_DIM = 64
NSA_HEADS = 8
NSA_KV_HEADS = 2
NSA_GROUP = NSA_HEADS // NSA_KV_HEADS
SB_HEADS = 8
CMP_BLOCK = 32
CMP_STRIDE = 16
SLC_BLOCK = 64
SLC_TOPK = 16
WINDOW = 512
WIN_QBLOCK = 128
SLC_QCHUNK = 64
SB_QBLOCK = 128
D_FF = 2816
CONV_WIDTH = 3
ROPE_THETA = 10000.0
RMS_EPS = 1e-6
NEG_INF = -1e30
FORCE_BONUS = 1e4

NSA_WIDTH = NSA_HEADS * HEAD_DIM
KV_WIDTH = NSA_KV_HEADS * HEAD_DIM
SB_WIDTH = SB_HEADS * HEAD_DIM
IN_SPLITS = (NSA_WIDTH, 6 * KV_WIDTH, 3 * NSA_HEADS, 3 * SB_WIDTH, 2 * D_MODEL)
IN_WIDTH = sum(IN_SPLITS)

kernel_name = 'nsa_stickbreak_gated_hybrid'


def rms_norm(x, gain):
    xf = x.astype(jnp.float32)
    y = xf * lax.rsqrt(jnp.mean(xf * xf, axis=-1, keepdims=True) + RMS_EPS)
    return y.astype(x.dtype) * gain


def rope(x, positions):
    half = x.shape[-1] // 2
    inv_freq = ROPE_THETA ** (-jnp.arange(half, dtype=jnp.float32) / half)
    ang = positions.astype(jnp.float32)[:, :, None] * inv_freq
    cos = jnp.cos(ang)[:, :, None, :]
    sin = jnp.sin(ang)[:, :, None, :]
    xf = x.astype(jnp.float32)
    x1, x2 = xf[..., :half], xf[..., half:]
    return jnp.concatenate([x1 * cos - x2 * sin, x2 * cos + x1 * sin], axis=-1).astype(x.dtype)


def compress_blocks(k, pe, w1, b1, w2):
    B, T, Hkv, hd = k.shape
    n_cmp = (T - CMP_BLOCK) // CMP_STRIDE + 1
    idx = np.arange(n_cmp)[:, None] * CMP_STRIDE + np.arange(CMP_BLOCK)[None, :]
    blocks = k[:, idx] + pe[None, None, :, None, :]
    flat = jnp.transpose(blocks, (0, 1, 3, 2, 4)).reshape(B, n_cmp, Hkv, CMP_BLOCK * hd)
    return jax.nn.gelu(flat @ w1 + b1, approximate=True) @ w2


def nsa_attention(q, k_cmp, v_cmp, k_slc, v_slc, k_win, v_win, gate_logits,
                  ck_pe, ck_w1, ck_b1, ck_w2, cv_pe, cv_w1, cv_b1, cv_w2):
    B, T, H, hd = q.shape
    scale = hd ** -0.5
    qg = q.reshape(B, T, NSA_KV_HEADS, NSA_GROUP, hd)
    t_idx = jnp.arange(T)

    kc = compress_blocks(k_cmp, ck_pe, ck_w1, ck_b1, ck_w2)
    vc = compress_blocks(v_cmp, cv_pe, cv_w1, cv_b1, cv_w2)
    n_cmp = kc.shape[1]
    cmp_end = jnp.arange(n_cmp) * CMP_STRIDE + CMP_BLOCK - 1
    cmp_mask = cmp_end[None, :] <= t_idx[:, None]
    s = jnp.einsum('bthgd,bnhd->bhgtn', qg, kc).astype(jnp.float32) * scale
    p_cmp = jax.nn.softmax(jnp.where(cmp_mask, s, NEG_INF), axis=-1) * cmp_mask
    o_cmp = jnp.einsum('bhgtn,bnhd->bthgd', p_cmp.astype(vc.dtype), vc)

    n_slc = T // SLC_BLOCK
    top_n = min(SLC_TOPK, n_slc)
    ci = np.arange(n_cmp)[:, None] * CMP_STRIDE
    sj = np.arange(n_slc)[None, :] * SLC_BLOCK
    overlap = ((ci < sj + SLC_BLOCK) & (sj < ci + CMP_BLOCK)).astype(np.float32)
    p_slc = jnp.einsum('bhtn,nj->bhtj', p_cmp.sum(axis=2), jnp.asarray(overlap))
    blk = jnp.arange(n_slc)
    cur = t_idx // SLC_BLOCK
    blk_valid = blk[None, :] * SLC_BLOCK <= t_idx[:, None]
    forced = (blk[None, :] == 0) | (blk[None, :] == cur[:, None]) | (blk[None, :] == cur[:, None] - 1)
    score = jnp.where(blk_valid, p_slc + jnp.where(forced, FORCE_BONUS, 0.0), -1.0)
    top_score, top_idx = lax.top_k(score, top_n)
    top_valid = top_score >= 0.0

    ks_blk = jnp.transpose(k_slc.reshape(B, n_slc, SLC_BLOCK, NSA_KV_HEADS, hd), (0, 3, 1, 2, 4))
    vs_blk = jnp.transpose(v_slc.reshape(B, n_slc, SLC_BLOCK, NSA_KV_HEADS, hd), (0, 3, 1, 2, 4))
    b_ix = jnp.arange(B)[:, None, None, None]
    h_ix = jnp.arange(NSA_KV_HEADS)[None, :, None, None]
    n_chunk = T // SLC_QCHUNK

    def slc_chunk(args):
        qc, idx_c, ok_c, t_c = args
        kg = ks_blk[b_ix, h_ix, idx_c]
        vg = vs_blk[b_ix, h_ix, idx_c]
        tok = idx_c[..., None] * SLC_BLOCK + jnp.arange(SLC_BLOCK)
        mask = ok_c[..., None] & (tok <= t_c[None, None, :, None, None])
        sc = jnp.einsum('bqhgd,bhqnld->bhgqnl', qc, kg).astype(jnp.float32) * scale
        sc = jnp.where(mask[:, :, None], sc, NEG_INF)
        p = jax.nn.softmax(sc.reshape(*sc.shape[:4], -1), axis=-1).reshape(sc.shape)
        return jnp.einsum('bhgqnl,bhqnld->bqhgd', p.astype(vg.dtype), vg)

    xs = (jnp.swapaxes(qg.reshape(B, n_chunk, SLC_QCHUNK, NSA_KV_HEADS, NSA_GROUP, hd), 0, 1),
          jnp.moveaxis(top_idx.reshape(B, NSA_KV_HEADS, n_chunk, SLC_QCHUNK, top_n), 2, 0),
          jnp.moveaxis(top_valid.reshape(B, NSA_KV_HEADS, n_chunk, SLC_QCHUNK, top_n), 2, 0),
          t_idx.reshape(n_chunk, SLC_QCHUNK))
    o_slc = jnp.swapaxes(lax.map(slc_chunk, xs), 0, 1).reshape(B, T, NSA_KV_HEADS, NSA_GROUP, hd)

    n_wblk = T // WIN_QBLOCK
    span = WINDOW + WIN_QBLOCK
    kw_pad = jnp.pad(k_win, ((0, 0), (WINDOW, 0), (0, 0), (0, 0)))
    vw_pad = jnp.pad(v_win, ((0, 0), (WINDOW, 0), (0, 0), (0, 0)))

    def win_block(i):
        start = i * WIN_QBLOCK
        qb = lax.dynamic_slice_in_dim(qg, start, WIN_QBLOCK, axis=1)
        kb = lax.dynamic_slice_in_dim(kw_pad, start, span, axis=1)
        vb = lax.dynamic_slice_in_dim(vw_pad, start, span, axis=1)
        q_pos = start + jnp.arange(WIN_QBLOCK)
        k_pos = start - WINDOW + jnp.arange(span)
        diff = q_pos[:, None] - k_pos[None, :]
        mask = (diff >= 0) & (diff < WINDOW) & (k_pos[None, :] >= 0)
        sc = jnp.einsum('bqhgd,bkhd->bhgqk', qb, kb).astype(jnp.float32) * scale
        p = jax.nn.softmax(jnp.where(mask, sc, NEG_INF), axis=-1)
        return jnp.einsum('bhgqk,bkhd->bqhgd', p.astype(vb.dtype), vb)

    o_win = jnp.swapaxes(lax.map(win_block, jnp.arange(n_wblk)), 0, 1).reshape(B, T, NSA_KV_HEADS, NSA_GROUP, hd)

    g = jax.nn.sigmoid(gate_logits.reshape(B, T, NSA_KV_HEADS, NSA_GROUP, 3))
    o = g[..., 0:1] * o_cmp + g[..., 1:2] * o_slc + g[..., 2:3] * o_win
    return o.reshape(B, T, H * hd)


def stick_breaking_attention(q, k, v):
    B, T, H, hd = q.shape
    scale = hd ** -0.5
    outs = []
    for i in range(T // SB_QBLOCK):
        q0, q1 = i * SB_QBLOCK, (i + 1) * SB_QBLOCK
        z = jnp.einsum('bqhd,bkhd->bhqk', q[:, q0:q1], k[:, :q1]).astype(jnp.float32) * scale
        q_pos = jnp.arange(q0, q1)
        k_pos = jnp.arange(q1)
        strict = k_pos[None, :] < q_pos[:, None]
        log_1m = jnp.where(strict, -jax.nn.softplus(z), 0.0)
        later = lax.cumsum(log_1m, axis=3, reverse=True) - log_1m
        w = jnp.where(strict, jnp.exp(jax.nn.log_sigmoid(z) + later), 0.0)
        outs.append(jnp.einsum('bhqk,bkhd->bqhd', w.astype(v.dtype), v[:, :q1]))
    return jnp.concatenate(outs, axis=1)


def hybrid_mixer(h, positions, w_in, ck_pe, ck_w1, ck_b1, ck_w2, cv_pe, cv_w1, cv_b1, cv_w2,
                 w_proj_nsa, w_proj_sb, w_out):
    B, T, _ = h.shape
    cuts = np.cumsum(IN_SPLITS)[:-1].tolist()
    q_nsa, kv_nsa, g_nsa, qkv_sb, g_merge = jnp.split(h @ w_in, cuts, axis=-1)
    q_nsa = rope(q_nsa.reshape(B, T, NSA_HEADS, HEAD_DIM), positions)
    kv = kv_nsa.reshape(B, T, 6, NSA_KV_HEADS, HEAD_DIM)
    k_cmp = rope(kv[:, :, 0], positions)
    v_cmp = kv[:, :, 1]
    k_slc = rope(kv[:, :, 2], positions)
    v_slc = kv[:, :, 3]
    k_win = rope(kv[:, :, 4], positions)
    v_win = kv[:, :, 5]
    o_nsa = nsa_attention(q_nsa, k_cmp, v_cmp, k_slc, v_slc, k_win, v_win, g_nsa,
                          ck_pe, ck_w1, ck_b1, ck_w2, cv_pe, cv_w1, cv_b1, cv_w2)
    qkv = qkv_sb.reshape(B, T, 3, SB_HEADS, HEAD_DIM)
    o_sb = stick_breaking_attention(qkv[:, :, 0], qkv[:, :, 1], qkv[:, :, 2]).reshape(B, T, SB_WIDTH)
    g_a, g_b = jnp.split(jax.nn.sigmoid(g_merge), 2, axis=-1)
    y = g_a * (o_nsa @ w_proj_nsa) + g_b * (o_sb @ w_proj_sb)
    return y @ w_out


def conv_ffn(h, w_up, conv_w, conv_b, w_down):
    u = h @ w_up
    c = u.shape[-1]
    u = lax.conv_general_dilated(u, conv_w[:, None, :], window_strides=(1,),
                                 padding=[(CONV_WIDTH - 1, 0)],
                                 dimension_numbers=('NWC', 'WIO', 'NWC'),
                                 feature_group_count=c) + conv_b
    gate, up = jnp.split(u, 2, axis=-1)
    return (jax.nn.gelu(gate, approximate=True) * up) @ w_down


def setup_inputs(seed: int = 0) -> dict:
    key = jax.random.key(seed)
    ks = jax.random.split(key, 24)
    f32 = jnp.float32

    def nrm(k, shape, scale):
        return jax.random.normal(k, shape, f32) * scale

    def gain(k):
        return 1.0 + 0.02 * jax.random.normal(k, (DEPTH, D_MODEL), f32)

    cmp_in = CMP_BLOCK * HEAD_DIM
    return {
        'x': nrm(ks[0], (BATCH, SEQ, D_MODEL), 1.0),
        'positions': jnp.broadcast_to(jnp.arange(SEQ, dtype=jnp.int32), (BATCH, SEQ)),
        'g_pre_mix': gain(ks[1]),
        'w_in': nrm(ks[2], (DEPTH, D_MODEL, IN_WIDTH), D_MODEL ** -0.5),
        'ck_pe': nrm(ks[3], (DEPTH, CMP_BLOCK, HEAD_DIM), 0.02),
        'ck_w1': nrm(ks[4], (DEPTH, cmp_in, HEAD_DIM), cmp_in ** -0.5),
        'ck_b1': nrm(ks[5], (DEPTH, HEAD_DIM), 0.01),
        'ck_w2': nrm(ks[6], (DEPTH, HEAD_DIM, HEAD_DIM), HEAD_DIM ** -0.5),
        'cv_pe': nrm(ks[7], (DEPTH, CMP_BLOCK, HEAD_DIM), 0.02),
        'cv_w1': nrm(ks[8], (DEPTH, cmp_in, HEAD_DIM), cmp_in ** -0.5),
        'cv_b1': nrm(ks[9], (DEPTH, HEAD_DIM), 0.01),
        'cv_w2': nrm(ks[10], (DEPTH, HEAD_DIM, HEAD_DIM), HEAD_DIM ** -0.5),
        'w_proj_nsa': nrm(ks[11], (DEPTH, NSA_WIDTH, D_MODEL), NSA_WIDTH ** -0.5),
        'w_proj_sb': nrm(ks[12], (DEPTH, SB_WIDTH, D_MODEL), SB_WIDTH ** -0.5),
        'w_out': nrm(ks[13], (DEPTH, D_MODEL, D_MODEL), D_MODEL ** -0.5),
        'g_post_mix': gain(ks[14]),
        'g_pre_ffn': gain(ks[15]),
        'w_up': nrm(ks[16], (DEPTH, D_MODEL, 2 * D_FF), D_MODEL ** -0.5),
        'conv_w': nrm(ks[17], (DEPTH, CONV_WIDTH, 2 * D_FF), CONV_WIDTH ** -0.5),
        'conv_b': nrm(ks[18], (DEPTH, 2 * D_FF), 0.01),
        'w_down': nrm(ks[19], (DEPTH, D_FF, D_MODEL), D_FF ** -0.5),
        'g_post_ffn': gain(ks[20]),
    }


def reference(x, positions, g_pre_mix, w_in, ck_pe, ck_w1, ck_b1, ck_w2, cv_pe, cv_w1, cv_b1, cv_w2,
              w_proj_nsa, w_proj_sb, w_out, g_post_mix, g_pre_ffn, w_up, conv_w, conv_b, w_down, g_post_ffn):
    for l in range(DEPTH):
        h = rms_norm(x, g_pre_mix[l])
        m = hybrid_mixer(h, positions, w_in[l], ck_pe[l], ck_w1[l], ck_b1[l], ck_w2[l],
                         cv_pe[l], cv_w1[l], cv_b1[l], cv_w2[l], w_proj_nsa[l], w_proj_sb[l], w_out[l])
        x = x + rms_norm(m, g_post_mix[l])
        h = rms_norm(x, g_pre_ffn[l])
        f = conv_ffn(h, w_up[l], conv_w[l], conv_b[l], w_down[l])
        x = x + rms_norm(f, g_post_ffn[l])
    return x
```

```python
import functools

import numpy as np
import jax
import jax.numpy as jnp
from jax import lax
from jax.experimental import pallas as pl
from jax.experimental.pallas import tpu as pltpu

D_MODEL = 1024
HEAD_DIM = 64
NSA_HEADS = 8
NSA_KV_HEADS = 2
NSA_GROUP = NSA_HEADS // NSA_KV_HEADS
SB_HEADS = 8
CMP_BLOCK = 32
CMP_STRIDE = 16
SLC_BLOCK = 64
SLC_TOPK = 16
WINDOW = 512
D_FF = 2816
ROPE_THETA = 10000.0
RMS_EPS = 1e-6
NEG_INF = -1e30
FORCE_BONUS = 1e4

NSA_WIDTH = NSA_HEADS * HEAD_DIM
KV_WIDTH = NSA_KV_HEADS * HEAD_DIM
SB_WIDTH = SB_HEADS * HEAD_DIM
N_GATE = 3 * NSA_HEADS

LANES = 128
VMEM_LIMIT = 56 * 1024 * 1024

COL_Q = 0
COL_KV = NSA_WIDTH
COL_SB = COL_KV + 6 * KV_WIDTH
PROJ_WIDTH = COL_SB + 3 * SB_WIDTH
W_MAIN_WIDTH = PROJ_WIDTH + LANES

TM_IN = 512
TQ_NSA = 128
TK_SLC = 256
TQ_SB = 256
TM_MIX = 512
TM_FFN = 512
FF_CHUNK = 256
HALO = 8


def _dot(a, b):
    return jnp.dot(a, b, preferred_element_type=jnp.float32)


def _dot_nt(a, b):
    return lax.dot_general(a, b, (((1,), (1,)), ((), ())), preferred_element_type=jnp.float32)


def _rms_norm(xf, gain):
    y = xf * lax.rsqrt(jnp.mean(xf * xf, axis=-1, keepdims=True) + RMS_EPS)
    return y * gain


def _rope_blocks(y, cos, sin_signed, n_blocks, rope_block):
    lane = lax.broadcasted_iota(jnp.int32, (1, LANES), 1)
    first_half = (lane % HEAD_DIM) < (HEAD_DIM // 2)
    outs = []
    for b in range(n_blocks):
        blk = y[:, b * LANES:(b + 1) * LANES]
        if rope_block(b):
            partner = jnp.where(first_half,
                                pltpu.roll(blk, LANES - HEAD_DIM // 2, 1),
                                pltpu.roll(blk, HEAD_DIM // 2, 1))
            blk = blk * cos + partner * sin_signed
        outs.append(blk)
    return outs


def _in_proj_kernel(x_ref, pos_ref, g_ref, w_ref, freq_ref, sign_ref, proj_ref, gate_ref):
    scale = HEAD_DIM ** -0.5
    h = _rms_norm(x_ref[...], g_ref[...]).astype(jnp.bfloat16)
    ang = pos_ref[...].astype(jnp.float32) * freq_ref[...]
    cos = jnp.cos(ang)
    sin_signed = jnp.sin(ang) * sign_ref[...]

    yq = _dot(h, w_ref[:, COL_Q:COL_Q + NSA_WIDTH])
    for b, blk in enumerate(_rope_blocks(yq, cos, sin_signed, NSA_WIDTH // LANES, lambda b: True)):
        proj_ref[:, COL_Q + b * LANES:COL_Q + (b + 1) * LANES] = (blk * scale).astype(proj_ref.dtype)

    ykv = _dot(h, w_ref[:, COL_KV:COL_KV + 6 * KV_WIDTH])
    for b, blk in enumerate(_rope_blocks(ykv, cos, sin_signed, 6, lambda b: b % 2 == 0)):
        proj_ref[:, COL_KV + b * LANES:COL_KV + (b + 1) * LANES] = blk.astype(proj_ref.dtype)

    ysb = _dot(h, w_ref[:, COL_SB:COL_SB + 3 * SB_WIDTH])
    proj_ref[:, COL_SB:COL_SB + SB_WIDTH] = (ysb[:, :SB_WIDTH] * scale).astype(proj_ref.dtype)
    proj_ref[:, COL_SB + SB_WIDTH:PROJ_WIDTH] = ysb[:, SB_WIDTH:].astype(proj_ref.dtype)

    gate_ref[...] = jax.nn.sigmoid(_dot(h, w_ref[:, PROJ_WIDTH:W_MAIN_WIDTH]))


def _in_proj(x2d, pos2d, gain, w_main, freq, sign):
    n = x2d.shape[0]
    tm = TM_IN
    const = lambda i: (0, 0)
    return pl.pallas_call(
        _in_proj_kernel,
        grid=(n // tm,),
        in_specs=[
            pl.BlockSpec((tm, D_MODEL), lambda i: (i, 0)),
            pl.BlockSpec((tm, 1), lambda i: (i, 0)),
            pl.BlockSpec((1, D_MODEL), const),
            pl.BlockSpec((D_MODEL, W_MAIN_WIDTH), const),
            pl.BlockSpec((1, LANES), const),
            pl.BlockSpec((1, LANES), const),
        ],
        out_specs=[
            pl.BlockSpec((tm, PROJ_WIDTH), lambda i: (i, 0)),
            pl.BlockSpec((tm, LANES), lambda i: (i, 0)),
        ],
        out_shape=[
            jax.ShapeDtypeStruct((n, PROJ_WIDTH), jnp.bfloat16),
            jax.ShapeDtypeStruct((n, LANES), jnp.float32),
        ],
        compiler_params=pltpu.CompilerParams(
            dimension_semantics=("parallel",), vmem_limit_bytes=VMEM_LIMIT),
        name="in_proj",
    )(x2d, pos2d, gain, w_main, freq, sign)


def _compress_kernel(x_ref, pe_ref, w1_ref, b1_ref, w2_ref, out_ref):
    half = CMP_STRIDE * HEAD_DIM
    for j in range(2 * NSA_KV_HEADS):
        p = j // NSA_KV_HEADS
        x = x_ref[0, j]
        w1 = w1_ref[p]
        top = _dot(x, w1[:half])
        bot = _dot(x, w1[half:])
        n_chunk = x.shape[0]
        bot_next = pltpu.roll(bot, n_chunk - 1, 0)
        pe_row = jnp.broadcast_to(pe_ref[p], (8, 2 * half)).astype(jnp.bfloat16)
        bias = _dot(pe_row, w1)[0:1] + b1_ref[p]
        hid = jax.nn.gelu(top + bot_next + bias, approximate=True)
        out_ref[0, j] = _dot(hid.astype(jnp.bfloat16), w2_ref[p])


def _compress(chunks, pe, w1, b1, w2):
    b, four, n_chunk, width = chunks.shape
    c3 = lambda i: (0, 0, 0)
    return pl.pallas_call(
        _compress_kernel,
        grid=(b,),
        in_specs=[
            pl.BlockSpec((1, four, n_chunk, width), lambda i: (i, 0, 0, 0)),
            pl.BlockSpec(pe.shape, c3),
            pl.BlockSpec(w1.shape, c3),
            pl.BlockSpec(b1.shape, c3),
            pl.BlockSpec(w2.shape, c3),
        ],
        out_specs=pl.BlockSpec((1, four, n_chunk, HEAD_DIM), lambda i: (i, 0, 0, 0)),
        out_shape=jax.ShapeDtypeStruct((b, four, n_chunk, HEAD_DIM), jnp.float32),
        compiler_params=pltpu.CompilerParams(
            dimension_semantics=("parallel",), vmem_limit_bytes=VMEM_LIMIT),
        name="compress",
    )(chunks, pe, w1, b1, w2)


def _masked_softmax_rows(s, mask):
    m = jnp.max(jnp.where(mask, s, NEG_INF), axis=-1, keepdims=True)
    e = jnp.where(mask, jnp.exp(s - m), 0.0)
    den = jnp.sum(e, axis=-1, keepdims=True)
    return e / jnp.where(den > 0.0, den, 1.0)


def _nsa_kernel(q_ref, kvs_ref, kvw_ref, cmp_ref, gate_ref, ovl_ref, out_ref, *, seq_len):
    tq = TQ_NSA
    g = NSA_GROUP
    i = pl.program_id(1)
    t0 = i * tq
    n_cmp_pad = cmp_ref.shape[2]
    n_cmp = (seq_len - CMP_BLOCK) // CMP_STRIDE + 1
    n_slc = seq_len // SLC_BLOCK
    span = WINDOW + tq

    t_col = t0 + lax.broadcasted_iota(jnp.int32, (tq, 1), 0)
    gates = gate_ref[0]

    for hk in range(NSA_KV_HEADS):
        q_st = jnp.concatenate(
            [q_ref[0, :, (hk * g + gi) * HEAD_DIM:(hk * g + gi + 1) * HEAD_DIM] for gi in range(g)],
            axis=0)

        kc = cmp_ref[0, hk].astype(jnp.bfloat16)
        vc = cmp_ref[0, NSA_KV_HEADS + hk].astype(jnp.bfloat16)
        s_c = _dot_nt(q_st, kc)
        n_lane = lax.broadcasted_iota(jnp.int32, (1, n_cmp_pad), 1)
        cmask1 = (n_lane * CMP_STRIDE + (CMP_BLOCK - 1) <= t_col) & (n_lane < n_cmp)
        cmask = jnp.concatenate([cmask1] * g, axis=0)
        p_c = _masked_softmax_rows(s_c, cmask)
        o_cmp = _dot(p_c.astype(jnp.bfloat16), vc)

        p_sum = p_c[0:tq]
        for gi in range(1, g):
            p_sum = p_sum + p_c[gi * tq:(gi + 1) * tq]
        p_slc = jnp.dot(p_sum, ovl_ref[...], precision=lax.Precision.HIGHEST,
                        preferred_element_type=jnp.float32)
        blk = lax.broadcasted_iota(jnp.int32, (1, n_slc), 1)
        cur = t_col // SLC_BLOCK
        valid = blk * SLC_BLOCK <= t_col
        forced = (blk == 0) | (blk == cur) | (blk == cur - 1)
        score = jnp.where(valid, p_slc + jnp.where(forced, FORCE_BONUS, 0.0), -1.0)
        rank = jnp.zeros((tq, n_slc), jnp.float32)
        for c in range(n_slc):
            col = score[:, c:c + 1]
            beats = (col > score) | ((col == score) & (blk > c))
            rank = rank + beats.astype(jnp.float32)
        top_n = min(SLC_TOPK, n_slc)
        sel = ((rank < top_n) & (score >= 0.0)).astype(jnp.bfloat16)

        def slc_step(c, carry):
            m_prev, l_prev, acc = carry
            k0 = pl.multiple_of(c * TK_SLC, TK_SLC)
            kb = kvs_ref[0, pl.ds(k0, TK_SLC), hk * HEAD_DIM:(hk + 1) * HEAD_DIM]
            vb = kvs_ref[0, pl.ds(k0, TK_SLC), KV_WIDTH + hk * HEAD_DIM:KV_WIDTH + (hk + 1) * HEAD_DIM]
            s = _dot_nt(q_st, kb)
            tok = k0 + lax.broadcasted_iota(jnp.int32, (1, TK_SLC), 1)
            expand = (lax.broadcasted_iota(jnp.int32, (n_slc, TK_SLC), 0)
                      == (k0 + lax.broadcasted_iota(jnp.int32, (n_slc, TK_SLC), 1)) // SLC_BLOCK)
            chosen = _dot(sel, expand.astype(jnp.bfloat16)) > 0.5
            mask1 = chosen & (tok <= t_col)
            mask = jnp.concatenate([mask1] * g, axis=0)
            m_new = jnp.maximum(m_prev, jnp.max(jnp.where(mask, s, NEG_INF), axis=-1, keepdims=True))
            alpha = jnp.exp(m_prev - m_new)
            p = jnp.where(mask, jnp.exp(s - m_new), 0.0)
            l_new = alpha * l_prev + jnp.sum(p, axis=-1, keepdims=True)
            acc = alpha * acc + _dot(p.astype(jnp.bfloat16), vb)
            return m_new, l_new, acc

        n_chunks = (t0 + tq + TK_SLC - 1) // TK_SLC
        init = (jnp.full((g * tq, 1), NEG_INF, jnp.float32),
                jnp.zeros((g * tq, 1), jnp.float32),
                jnp.zeros((g * tq, HEAD_DIM), jnp.float32))
        _, l_s, acc_s = lax.fori_loop(0, n_chunks, slc_step, init)
        o_slc = acc_s / l_s

        start = pl.multiple_of(jnp.maximum(t0 - WINDOW, 0), tq)
        kw = kvw_ref[0, pl.ds(start, span), hk * HEAD_DIM:(hk + 1) * HEAD_DIM]
        vw = kvw_ref[0, pl.ds(start, span), KV_WIDTH + hk * HEAD_DIM:KV_WIDTH + (hk + 1) * HEAD_DIM]
        s_w = _dot_nt(q_st, kw)
        diff = t_col - (start + lax.broadcasted_iota(jnp.int32, (1, span), 1))
        wmask1 = (diff >= 0) & (diff < WINDOW)
        p_w = _masked_softmax_rows(s_w, jnp.concatenate([wmask1] * g, axis=0))
        o_win = _dot(p_w.astype(jnp.bfloat16), vw)

        for gi in range(g):
            hd = hk * g + gi
            rows = slice(gi * tq, (gi + 1) * tq)
            o = (gates[:, 3 * hd:3 * hd + 1] * o_cmp[rows]
                 + gates[:, 3 * hd + 1:3 * hd + 2] * o_slc[rows]
                 + gates[:, 3 * hd + 2:3 * hd + 3] * o_win[rows])
            out_ref[0, :, hd * HEAD_DIM:(hd + 1) * HEAD_DIM] = o.astype(out_ref.dtype)


def _nsa(proj, cmp_kv, gates, overlap):
    b, t, _ = proj.shape
    tq = TQ_NSA
    pair = 2 * KV_WIDTH
    return pl.pallas_call(
        functools.partial(_nsa_kernel, seq_len=t),
        grid=(b, t // tq),
        in_specs=[
            pl.BlockSpec((1, tq, NSA_WIDTH), lambda bi, i: (bi, i, COL_Q // NSA_WIDTH)),
            pl.BlockSpec((1, t, pair), lambda bi, i: (bi, 0, (COL_KV + pair) // pair)),
            pl.BlockSpec((1, t, pair), lambda bi, i: (bi, 0, (COL_KV + 2 * pair) // pair)),
            pl.BlockSpec((1,) + cmp_kv.shape[1:], lambda bi, i: (bi, 0, 0, 0)),
            pl.BlockSpec((1, tq, LANES), lambda bi, i: (bi, i, 0)),
            pl.BlockSpec(overlap.shape, lambda bi, i: (0, 0)),
        ],
        out_specs=pl.BlockSpec((1, tq, NSA_WIDTH), lambda bi, i: (bi, i, 0)),
        out_shape=jax.ShapeDtypeStruct((b, t, NSA_WIDTH), jnp.bfloat16),
        compiler_params=pltpu.CompilerParams(
            dimension_semantics=("parallel", "arbitrary"), vmem_limit_bytes=VMEM_LIMIT),
        name="nsa",
    )(proj, proj, proj, cmp_kv, gates, overlap)


def _sb_kernel(q_ref, k_ref, v_ref, out_ref):
    tq = TQ_SB
    i = pl.program_id(2)
    row = lax.broadcasted_iota(jnp.int32, (tq, tq), 0)
    col = lax.broadcasted_iota(jnp.int32, (tq, tq), 1)
    upper = (row > col).astype(jnp.bfloat16)

    for hh in range(LANES // HEAD_DIM):
        lanes = slice(hh * HEAD_DIM, (hh + 1) * HEAD_DIM)
        q = q_ref[0, :, lanes]

        def step(it, carry_acc):
            carry, acc = carry_acc
            kb = i - it
            k0 = pl.multiple_of(kb * tq, tq)
            k = k_ref[0, pl.ds(k0, tq), lanes]
            v = v_ref[0, pl.ds(k0, tq), lanes]
            z = _dot_nt(q, k)
            strict = (k0 + col) < (i * tq + row)
            sp = jnp.maximum(z, 0.0) + jnp.log1p(jnp.exp(-jnp.abs(z)))
            log_1m = jnp.where(strict, -sp, 0.0)
            hi = log_1m.astype(jnp.bfloat16)
            lo = (log_1m - hi.astype(jnp.float32)).astype(jnp.bfloat16)
            later = _dot(hi, upper) + _dot(lo, upper) + carry
            w = jnp.where(strict, jnp.exp(z - sp + later), 0.0)
            acc = acc + _dot(w.astype(jnp.bfloat16), v)
            carry = carry + jnp.sum(log_1m, axis=-1, keepdims=True)
            return carry, acc

        init = (jnp.zeros((tq, 1), jnp.float32), jnp.zeros((tq, HEAD_DIM), jnp.float32))
        _, acc = lax.fori_loop(0, i + 1, step, init)
        out_ref[0, :, lanes] = acc.astype(out_ref.dtype)


def _sb(proj):
    b, t, _ = proj.shape
    tq = TQ_SB
    pairs = SB_WIDTH // LANES
    q0, k0, v0 = COL_SB // LANES, (COL_SB + SB_WIDTH) // LANES, (COL_SB + 2 * SB_WIDTH) // LANES
    return pl.pallas_call(
        _sb_kernel,
        grid=(b, pairs, t // tq),
        in_specs=[
            pl.BlockSpec((1, tq, LANES), lambda bi, hp, i: (bi, i, q0 + hp)),
            pl.BlockSpec((1, t, LANES), lambda bi, hp, i: (bi, 0, k0 + hp)),
            pl.BlockSpec((1, t, LANES), lambda bi, hp, i: (bi, 0, v0 + hp)),
        ],
        out_specs=pl.BlockSpec((1, tq, LANES), lambda bi, hp, i: (bi, i, hp)),
        out_shape=jax.ShapeDtypeStruct((b, t, SB_WIDTH), jnp.bfloat16),
        compiler_params=pltpu.CompilerParams(
            dimension_semantics=("parallel", "parallel", "arbitrary"), vmem_limit_bytes=VMEM_LIMIT),
        name="sb",
    )(proj, proj, proj)


def _mix_kernel(x_ref, on_ref, os_ref, gpre_ref, wg_ref, wpn_ref, wps_ref, wo_ref, gpost_ref, out_ref):
    x = x_ref[...]
    h = _rms_norm(x, gpre_ref[...]).astype(jnp.bfloat16)
    gm = jax.nn.sigmoid(_dot(h, wg_ref[...]))
    y = gm[:, :D_MODEL] * _dot(on_ref[...], wpn_ref[...]) + gm[:, D_MODEL:] * _dot(os_ref[...], wps_ref[...])
    m = _dot(y.astype(jnp.bfloat16), wo_ref[...])
    out_ref[...] = x + _rms_norm(m, gpost_ref[...])


def _mix(x2d, o_nsa, o_sb, g_pre, w_gm, w_pn, w_ps, w_out, g_post):
    n = x2d.shape[0]
    tm = TM_MIX
    const = lambda i: (0, 0)
    row = lambda i: (i, 0)
    return pl.pallas_call(
        _mix_kernel,
        grid=(n // tm,),
        in_specs=[
            pl.BlockSpec((tm, D_MODEL), row),
            pl.BlockSpec((tm, NSA_WIDTH), row),
            pl.BlockSpec((tm, SB_WIDTH), row),
            pl.BlockSpec((1, D_MODEL), const),
            pl.BlockSpec(w_gm.shape, const),
            pl.BlockSpec(w_pn.shape, const),
            pl.BlockSpec(w_ps.shape, const),
            pl.BlockSpec(w_out.shape, const),
            pl.BlockSpec((1, D_MODEL), const),
        ],
        out_specs=pl.BlockSpec((tm, D_MODEL), row),
        out_shape=jax.ShapeDtypeStruct((n, D_MODEL), jnp.float32),
        compiler_params=pltpu.CompilerParams(
            dimension_semantics=("parallel",), vmem_limit_bytes=VMEM_LIMIT),
        name="mix",
    )(x2d, o_nsa, o_sb, g_pre, w_gm, w_pn, w_ps, w_out, g_post)


def _ffn_kernel(x_ref, gpre_ref, wup_ref, cw_ref, cb_ref, wdn_ref, gpost_ref, out_ref, halo_ref):
    tm = TM_FFN
    ti = pl.program_id(1)

    @pl.when(ti == 0)
    def _():
        halo_ref[...] = jnp.zeros_like(halo_ref)

    x = x_ref[0]
    h = _rms_norm(x, gpre_ref[...]).astype(jnp.bfloat16)
    row = lax.broadcasted_iota(jnp.int32, (tm, 1), 0)

    def conv(u, c0):
        cols = slice(c0, c0 + FF_CHUNK)
        prev = halo_ref[:, cols]
        halo_ref[:, cols] = u[tm - HALO:]
        u1 = jnp.where(row == 0, prev[HALO - 1:HALO], pltpu.roll(u, 1, 0))
        u2 = jnp.where(row == 0, prev[HALO - 2:HALO - 1],
                       jnp.where(row == 1, prev[HALO - 1:HALO], pltpu.roll(u, 2, 0)))
        return cw_ref[0:1, cols] * u2 + cw_ref[1:2, cols] * u1 + cw_ref[2:3, cols] * u + cb_ref[:, cols]

    f = jnp.zeros((tm, D_MODEL), jnp.float32)
    for c in range(D_FF // FF_CHUNK):
        g0 = c * FF_CHUNK
        u0 = D_FF + c * FF_CHUNK
        gate = conv(_dot(h, wup_ref[:, g0:g0 + FF_CHUNK]), g0)
        up = conv(_dot(h, wup_ref[:, u0:u0 + FF_CHUNK]), u0)
        act = (jax.nn.gelu(gate, approximate=True) * up).astype(jnp.bfloat16)
        f = f + _dot(act, wdn_ref[g0:g0 + FF_CHUNK, :])
    out_ref[0] = x + _rms_norm(f, gpost_ref[...])


def _ffn(x, g_pre, w_up, conv_w, conv_b, w_down, g_post):
    b, t, _ = x.shape
    tm = TM_FFN
    const = lambda bi, i: (0, 0)
    return pl.pallas_call(
        _ffn_kernel,
        grid=(b, t // tm),
        in_specs=[
            pl.BlockSpec((1, tm, D_MODEL), lambda bi, i: (bi, i, 0)),
            pl.BlockSpec((1, D_MODEL), const),
            pl.BlockSpec(w_up.shape, const),
            pl.BlockSpec(conv_w.shape, const),
            pl.BlockSpec(conv_b.shape, const),
            pl.BlockSpec(w_down.shape, const),
            pl.BlockSpec((1, D_MODEL), const),
        ],
        out_specs=pl.BlockSpec((1, tm, D_MODEL), lambda bi, i: (bi, i, 0)),
        out_shape=jax.ShapeDtypeStruct(x.shape, jnp.float32),
        scratch_shapes=[pltpu.VMEM((HALO, 2 * D_FF), jnp.float32)],
        compiler_params=pltpu.CompilerParams(
            dimension_semantics=("parallel", "arbitrary"), vmem_limit_bytes=VMEM_LIMIT),
        name="ffn",
    )(x, g_pre, w_up, conv_w, conv_b, w_down, g_post)


def _overlap_matrix(seq_len, n_rows):
    n_cmp = (seq_len - CMP_BLOCK) // CMP_STRIDE + 1
    n_slc = seq_len // SLC_BLOCK
    ci = np.arange(n_cmp)[:, None] * CMP_STRIDE
    sj = np.arange(n_slc)[None, :] * SLC_BLOCK
    ovl = np.zeros((n_rows, n_slc), np.float32)
    ovl[:n_cmp] = ((ci < sj + SLC_BLOCK) & (sj < ci + CMP_BLOCK)).astype(np.float32)
    return jnp.asarray(ovl)


def kernel(x, positions, g_pre_mix, w_in, ck_pe, ck_w1, ck_b1, ck_w2, cv_pe, cv_w1, cv_b1, cv_w2,
           w_proj_nsa, w_proj_sb, w_out, g_post_mix, g_pre_ffn, w_up, conv_w, conv_b, w_down, g_post_ffn):
    b, t, d = x.shape
    depth = w_in.shape[0]
    assert d == D_MODEL and t % TM_FFN == 0 and t % TQ_SB == 0 and t >= WINDOW + TQ_NSA
    assert t % (CMP_STRIDE * 8) == 0 and t // SLC_BLOCK <= LANES
    bf = jnp.bfloat16
    n_chunk = t // CMP_STRIDE

    half = HEAD_DIM // 2
    inv_freq = ROPE_THETA ** (-jnp.arange(half, dtype=jnp.float32) / half)
    freq = jnp.tile(inv_freq, LANES // half)[None, :]
    sign = jnp.asarray(np.tile(np.concatenate([-np.ones(half), np.ones(half)]), LANES // HEAD_DIM),
                       jnp.float32)[None, :]
    overlap = _overlap_matrix(t, n_chunk)
    pos2d = positions.reshape(b * t, 1)

    c_gate = NSA_WIDTH + 6 * KV_WIDTH
    c_sb = c_gate + N_GATE
    c_merge = c_sb + 3 * SB_WIDTH

    for l in range(depth):
        wl = w_in[l]
        w_main = jnp.concatenate(
            [wl[:, :c_gate], wl[:, c_sb:c_merge], wl[:, c_gate:c_sb],
             jnp.zeros((d, LANES - N_GATE), wl.dtype)], axis=1).astype(bf)
        w_gm = wl[:, c_merge:].astype(bf)

        proj, gates = _in_proj(x.reshape(b * t, d), pos2d, g_pre_mix[l][None, :], w_main, freq, sign)
        proj = proj.reshape(b, t, PROJ_WIDTH)
        gates = gates.reshape(b, t, LANES)

        chunks = proj[:, :, COL_KV:COL_KV + 2 * KV_WIDTH].reshape(b, t, 2 * NSA_KV_HEADS, HEAD_DIM)
        chunks = jnp.transpose(chunks, (0, 2, 1, 3)).reshape(b, 2 * NSA_KV_HEADS, n_chunk, CMP_STRIDE * HEAD_DIM)
        cmp_kv = _compress(
            chunks,
            jnp.stack([ck_pe[l].reshape(1, -1), cv_pe[l].reshape(1, -1)]),
            jnp.stack([ck_w1[l], cv_w1[l]]).astype(bf),
            jnp.stack([ck_b1[l][None, :], cv_b1[l][None, :]]),
            jnp.stack([ck_w2[l], cv_w2[l]]).astype(bf))

        o_nsa = _nsa(proj, cmp_kv, gates, overlap)
        o_sb = _sb(proj)

        x = _mix(x.reshape(b * t, d), o_nsa.reshape(b * t, NSA_WIDTH), o_sb.reshape(b * t, SB_WIDTH),
                 g_pre_mix[l][None, :], w_gm, w_proj_nsa[l].astype(bf), w_proj_sb[l].astype(bf),
                 w_out[l].astype(bf), g_post_mix[l][None, :]).reshape(b, t, d)
        x = _ffn(x, g_pre_ffn[l][None, :], w_up[l].astype(bf), conv_w[l], conv_b[l][None, :],
                 w_down[l].astype(bf), g_post_ffn[l][None, :])
    return x
```

```python
import functools

import numpy as np
import jax
import jax.numpy as jnp
from jax import lax
from jax.experimental import pallas as pl
from jax.experimental.pallas import tpu as pltpu

D_MODEL = 1024
HEAD_DIM = 64
NSA_HEADS = 8
NSA_KV_HEADS = 2
NSA_GROUP = NSA_HEADS // NSA_KV_HEADS
SB_HEADS = 8
CMP_BLOCK = 32
CMP_STRIDE = 16
SLC_BLOCK = 64
SLC_TOPK = 16
WINDOW = 512
D_FF = 2816
ROPE_THETA = 10000.0
RMS_EPS = 1e-6
NEG_INF = -1e30
FORCE_BONUS = 1e4

NSA_WIDTH = NSA_HEADS * HEAD_DIM
KV_WIDTH = NSA_KV_HEADS * HEAD_DIM
SB_WIDTH = SB_HEADS * HEAD_DIM
N_GATE = 3 * NSA_HEADS

LANES = 128
VMEM_LIMIT = 56 * 1024 * 1024

COL_SB = 0
COL_Q = COL_SB + 3 * SB_WIDTH
COL_KV = COL_Q + NSA_WIDTH
PROJ_WIDTH = COL_KV + 6 * KV_WIDTH
W_MAIN_WIDTH = PROJ_WIDTH + LANES

TM_IN = 512
TQ_NSA = 128
TK_SLC = 256
TQ_SB = 256
SB_GROUP_WIDTH = 512
TM_MIX = 512
TM_FFN = 512
FF_CHUNK = 256
HALO = 8

assert COL_SB % SB_GROUP_WIDTH == 0 and SB_WIDTH % SB_GROUP_WIDTH == 0
assert COL_Q % NSA_WIDTH == 0 and COL_KV % (2 * KV_WIDTH) == 0


def _dot(a, b):
    return jnp.dot(a, b, preferred_element_type=jnp.float32)


def _dot_nt(a, b):
    return lax.dot_general(a, b, (((1,), (1,)), ((), ())), preferred_element_type=jnp.float32)


def _rms_norm(xf, gain):
    y = xf * lax.rsqrt(jnp.mean(xf * xf, axis=-1, keepdims=True) + RMS_EPS)
    return y * gain


def _rope_blocks(y, cos, sin_signed, n_blocks, rope_block):
    lane = lax.broadcasted_iota(jnp.int32, (1, LANES), 1)
    first_half = (lane % HEAD_DIM) < (HEAD_DIM // 2)
    outs = []
    for b in range(n_blocks):
        blk = y[:, b * LANES:(b + 1) * LANES]
        if rope_block(b):
            partner = jnp.where(first_half,
                                pltpu.roll(blk, LANES - HEAD_DIM // 2, 1),
                                pltpu.roll(blk, HEAD_DIM // 2, 1))
            blk = blk * cos + partner * sin_signed
        outs.append(blk)
    return outs


def _in_proj_kernel(x_ref, pos_ref, g_ref, w_ref, freq_ref, sign_ref, proj_ref, gate_ref):
    scale = HEAD_DIM ** -0.5
    h = _rms_norm(x_ref[...], g_ref[...]).astype(jnp.bfloat16)
    ang = pos_ref[...].astype(jnp.float32) * freq_ref[...]
    cos = jnp.cos(ang)
    sin_signed = jnp.sin(ang) * sign_ref[...]

    yq = _dot(h, w_ref[:, COL_Q:COL_Q + NSA_WIDTH])
    for b, blk in enumerate(_rope_blocks(yq, cos, sin_signed, NSA_WIDTH // LANES, lambda b: True)):
        proj_ref[:, COL_Q + b * LANES:COL_Q + (b + 1) * LANES] = (blk * scale).astype(proj_ref.dtype)

    ykv = _dot(h, w_ref[:, COL_KV:COL_KV + 6 * KV_WIDTH])
    for b, blk in enumerate(_rope_blocks(ykv, cos, sin_signed, 6, lambda b: b % 2 == 0)):
        proj_ref[:, COL_KV + b * LANES:COL_KV + (b + 1) * LANES] = blk.astype(proj_ref.dtype)

    ysb = _dot(h, w_ref[:, COL_SB:COL_SB + 3 * SB_WIDTH])
    proj_ref[:, COL_SB:COL_SB + SB_WIDTH] = (ysb[:, :SB_WIDTH] * scale).astype(proj_ref.dtype)
    proj_ref[:, COL_SB + SB_WIDTH:COL_SB + 3 * SB_WIDTH] = ysb[:, SB_WIDTH:].astype(proj_ref.dtype)

    gate_ref[...] = jax.nn.sigmoid(_dot(h, w_ref[:, PROJ_WIDTH:W_MAIN_WIDTH]))


def _in_proj(x2d, pos2d, gain, w_main, freq, sign):
    n = x2d.shape[0]
    tm = TM_IN
    const = lambda i: (0, 0)
    return pl.pallas_call(
        _in_proj_kernel,
        grid=(n // tm,),
        in_specs=[
            pl.BlockSpec((tm, D_MODEL), lambda i: (i, 0)),
            pl.BlockSpec((tm, 1), lambda i: (i, 0)),
            pl.BlockSpec((1, D_MODEL), const),
            pl.BlockSpec((D_MODEL, W_MAIN_WIDTH), const),
            pl.BlockSpec((1, LANES), const),
            pl.BlockSpec((1, LANES), const),
        ],
        out_specs=[
            pl.BlockSpec((tm, PROJ_WIDTH), lambda i: (i, 0)),
            pl.BlockSpec((tm, LANES), lambda i: (i, 0)),
        ],
        out_shape=[
            jax.ShapeDtypeStruct((n, PROJ_WIDTH), jnp.bfloat16),
            jax.ShapeDtypeStruct((n, LANES), jnp.float32),
        ],
        compiler_params=pltpu.CompilerParams(
            dimension_semantics=("parallel",), vmem_limit_bytes=VMEM_LIMIT),
        name="in_proj",
    )(x2d, pos2d, gain, w_main, freq, sign)


def _compress_kernel(x_ref, pe_ref, w1_ref, b1_ref, w2_ref, out_ref):
    half = CMP_STRIDE * HEAD_DIM
    for j in range(2 * NSA_KV_HEADS):
        p = j // NSA_KV_HEADS
        x = x_ref[0, j]
        w1 = w1_ref[p]
        top = _dot(x, w1[:half])
        bot = _dot(x, w1[half:])
        n_chunk = x.shape[0]
        bot_next = pltpu.roll(bot, n_chunk - 1, 0)
        pe_row = jnp.broadcast_to(pe_ref[p], (8, 2 * half)).astype(jnp.bfloat16)
        bias = _dot(pe_row, w1)[0:1] + b1_ref[p]
        hid = jax.nn.gelu(top + bot_next + bias, approximate=True)
        out_ref[0, j] = _dot(hid.astype(jnp.bfloat16), w2_ref[p])


def _compress(chunks, pe, w1, b1, w2):
    b, four, n_chunk, width = chunks.shape
    c3 = lambda i: (0, 0, 0)
    return pl.pallas_call(
        _compress_kernel,
        grid=(b,),
        in_specs=[
            pl.BlockSpec((1, four, n_chunk, width), lambda i: (i, 0, 0, 0)),
            pl.BlockSpec(pe.shape, c3),
            pl.BlockSpec(w1.shape, c3),
            pl.BlockSpec(b1.shape, c3),
            pl.BlockSpec(w2.shape, c3),
        ],
        out_specs=pl.BlockSpec((1, four, n_chunk, HEAD_DIM), lambda i: (i, 0, 0, 0)),
        out_shape=jax.ShapeDtypeStruct((b, four, n_chunk, HEAD_DIM), jnp.float32),
        compiler_params=pltpu.CompilerParams(
            dimension_semantics=("parallel",), vmem_limit_bytes=VMEM_LIMIT),
        name="compress",
    )(chunks, pe, w1, b1, w2)


def _masked_softmax_rows(s, mask):
    m = jnp.max(jnp.where(mask, s, NEG_INF), axis=-1, keepdims=True)
    e = jnp.where(mask, jnp.exp(s - m), 0.0)
    den = jnp.sum(e, axis=-1, keepdims=True)
    return e / jnp.where(den > 0.0, den, 1.0)


def _nsa_kernel(q_ref, kvs_ref, kvw_ref, cmp_ref, gate_ref, ovlt_ref, out_ref, *, seq_len):
    tq = TQ_NSA
    g = NSA_GROUP
    heads = range(NSA_KV_HEADS)
    i = pl.program_id(1)
    t0 = i * tq
    n_cmp_pad = cmp_ref.shape[2]
    n_cmp = (seq_len - CMP_BLOCK) // CMP_STRIDE + 1
    n_slc = seq_len // SLC_BLOCK
    top_n = min(SLC_TOPK, n_slc)
    span = WINDOW + tq
    slc_shift = SLC_BLOCK.bit_length() - 1

    t_col = t0 + lax.broadcasted_iota(jnp.int32, (tq, 1), 0)
    t_row = t0 + lax.broadcasted_iota(jnp.int32, (1, tq), 1)
    gates = gate_ref[0]

    def k_lanes(hk):
        return slice(hk * HEAD_DIM, (hk + 1) * HEAD_DIM)

    def v_lanes(hk):
        return slice(KV_WIDTH + hk * HEAD_DIM, KV_WIDTH + (hk + 1) * HEAD_DIM)

    q_st = [jnp.concatenate(
        [q_ref[0, :, (hk * g + gi) * HEAD_DIM:(hk * g + gi + 1) * HEAD_DIM] for gi in range(g)], axis=0)
        for hk in heads]

    def tile_rows(a):
        return jnp.concatenate([a] * g, axis=0)

    n_lane = lax.broadcasted_iota(jnp.int32, (1, n_cmp_pad), 1)
    cmask = tile_rows((n_lane * CMP_STRIDE + (CMP_BLOCK - 1) <= t_col) & (n_lane < n_cmp))
    p_c = [_masked_softmax_rows(_dot_nt(q_st[hk], cmp_ref[0, hk].astype(jnp.bfloat16)), cmask)
           for hk in heads]
    o_cmp = [_dot(p_c[hk].astype(jnp.bfloat16), cmp_ref[0, NSA_KV_HEADS + hk].astype(jnp.bfloat16))
             for hk in heads]

    blk = lax.broadcasted_iota(jnp.int32, (n_slc, 1), 0)
    cur = jnp.right_shift(t_row, slc_shift)
    valid = blk * SLC_BLOCK <= t_row
    bonus = jnp.where((blk == 0) | (blk == cur) | (blk == cur - 1), FORCE_BONUS, 0.0)
    sel = []
    for hk in heads:
        p_sum = p_c[hk][0:tq]
        for gi in range(1, g):
            p_sum = p_sum + p_c[hk][gi * tq:(gi + 1) * tq]
        p_slc = lax.dot_general(ovlt_ref[...], p_sum, (((1,), (1,)), ((), ())),
                                precision=lax.Precision.HIGHEST,
                                preferred_element_type=jnp.float32)
        score = jnp.where(valid, p_slc + bonus, -1.0)
        rank = jnp.zeros((n_slc, tq), jnp.float32)
        for c in range(n_slc):
            r = score[c:c + 1, :]
            rank = rank + ((r > score) | ((r >= score) & (blk > c))).astype(jnp.float32)
        chosen = ((rank < top_n) & (score >= 0.0)).astype(jnp.float32)
        sel.append(chosen.T.astype(jnp.bfloat16))

    def slc_chunk(c, carry, causal):
        k0 = pl.multiple_of(c * TK_SLC, TK_SLC)
        lane_tok = k0 + lax.broadcasted_iota(jnp.int32, (n_slc, TK_SLC), 1)
        expand = (lax.broadcasted_iota(jnp.int32, (n_slc, TK_SLC), 0)
                  == jnp.right_shift(lane_tok, slc_shift)).astype(jnp.bfloat16)
        new = []
        for hk in heads:
            m_prev, l_prev, acc = carry[hk]
            kb = kvs_ref[0, pl.ds(k0, TK_SLC), k_lanes(hk)]
            vb = kvs_ref[0, pl.ds(k0, TK_SLC), v_lanes(hk)]
            keep = _dot(sel[hk], expand) > 0.5
            if causal:
                keep = keep & (lane_tok[0:1] <= t_col)
            s = _dot_nt(q_st[hk], kb) + tile_rows(jnp.where(keep, 0.0, NEG_INF))
            m_new = jnp.maximum(m_prev, jnp.max(s, axis=-1, keepdims=True))
            alpha = jnp.exp(m_prev - m_new)
            p = jnp.exp(s - m_new)
            l_new = alpha * l_prev + jnp.sum(p, axis=-1, keepdims=True)
            acc = alpha * acc + _dot(p.astype(jnp.bfloat16), vb)
            new.append((m_new, l_new, acc))
        return tuple(new)

    last = (t0 + tq - 1) // TK_SLC
    init = tuple((jnp.full((g * tq, 1), NEG_INF, jnp.float32),
                  jnp.zeros((g * tq, 1), jnp.float32),
                  jnp.zeros((g * tq, HEAD_DIM), jnp.float32)) for _ in heads)
    carry = lax.fori_loop(0, last, lambda c, st: slc_chunk(c, st, False), init)
    carry = slc_chunk(last, carry, True)
    o_slc = [carry[hk][2] / carry[hk][1] for hk in heads]

    start = pl.multiple_of(jnp.maximum(t0 - WINDOW, 0), tq)
    diff = t_col - (start + lax.broadcasted_iota(jnp.int32, (1, span), 1))
    wbias = tile_rows(jnp.where((diff >= 0) & (diff < WINDOW), 0.0, NEG_INF))
    o_win = []
    for hk in heads:
        kw = kvw_ref[0, pl.ds(start, span), k_lanes(hk)]
        vw = kvw_ref[0, pl.ds(start, span), v_lanes(hk)]
        s_w = _dot_nt(q_st[hk], kw) + wbias
        e_w = jnp.exp(s_w - jnp.max(s_w, axis=-1, keepdims=True))
        o_w = _dot(e_w.astype(jnp.bfloat16), vw)
        o_win.append(o_w / jnp.sum(e_w, axis=-1, keepdims=True))

    for hk in heads:
        for gi in range(g):
            hd = hk * g + gi
            rows = slice(gi * tq, (gi + 1) * tq)
            o = (gates[:, 3 * hd:3 * hd + 1] * o_cmp[hk][rows]
                 + gates[:, 3 * hd + 1:3 * hd + 2] * o_slc[hk][rows]
                 + gates[:, 3 * hd + 2:3 * hd + 3] * o_win[hk][rows])
            out_ref[0, :, hd * HEAD_DIM:(hd + 1) * HEAD_DIM] = o.astype(out_ref.dtype)


def _nsa(proj, cmp_kv, gates, overlap_t):
    b, t, _ = proj.shape
    tq = TQ_NSA
    pair = 2 * KV_WIDTH
    return pl.pallas_call(
        functools.partial(_nsa_kernel, seq_len=t),
        grid=(b, t // tq),
        in_specs=[
            pl.BlockSpec((1, tq, NSA_WIDTH), lambda bi, i: (bi, i, COL_Q // NSA_WIDTH)),
            pl.BlockSpec((1, t, pair), lambda bi, i: (bi, 0, (COL_KV + pair) // pair)),
            pl.BlockSpec((1, t, pair), lambda bi, i: (bi, 0, (COL_KV + 2 * pair) // pair)),
            pl.BlockSpec((1,) + cmp_kv.shape[1:], lambda bi, i: (bi, 0, 0, 0)),
            pl.BlockSpec((1, tq, LANES), lambda bi, i: (bi, i, 0)),
            pl.BlockSpec(overlap_t.shape, lambda bi, i: (0, 0)),
        ],
        out_specs=pl.BlockSpec((1, tq, NSA_WIDTH), lambda bi, i: (bi, i, 0)),
        out_shape=jax.ShapeDtypeStruct((b, t, NSA_WIDTH), jnp.bfloat16),
        compiler_params=pltpu.CompilerParams(
            dimension_semantics=("parallel", "arbitrary"), vmem_limit_bytes=VMEM_LIMIT),
        name="nsa",
    )(proj, proj, proj, cmp_kv, gates, overlap_t)


def _sb_kernel(q_ref, k_ref, v_ref, out_ref):
    tq = TQ_SB
    i = pl.program_id(2)
    n_heads = q_ref.shape[2] // HEAD_DIM
    row = lax.broadcasted_iota(jnp.int32, (tq, tq), 0)
    col = lax.broadcasted_iota(jnp.int32, (tq, tq), 1)
    upper = (row > col).astype(jnp.bfloat16)
    strict = col < row
    lanes = [slice(h * HEAD_DIM, (h + 1) * HEAD_DIM) for h in range(n_heads)]
    qs = [q_ref[0, :, ln] for ln in lanes]

    def block(kb, state, diagonal):
        k0 = pl.multiple_of(kb * tq, tq)
        zs = [_dot_nt(qs[h], k_ref[0, pl.ds(k0, tq), lanes[h]]) for h in range(n_heads)]
        sps, logits = [], []
        for z in zs:
            sp = jnp.maximum(z, 0.0) + jnp.log(1.0 + jnp.exp(-jnp.abs(z)))
            if diagonal:
                sp = jnp.where(strict, sp, 0.0)
            sps.append(sp)
            logits.append(z - sp)
        later = _dot(jnp.concatenate([sp.astype(jnp.bfloat16) for sp in sps], axis=0), upper)
        new_state = []
        for h in range(n_heads):
            carry, acc = state[h]
            later_h = later[h * tq:(h + 1) * tq]
            w = jnp.exp(logits[h] - later_h - carry)
            if diagonal:
                w = jnp.where(strict, w, 0.0)
            acc = acc + _dot(w.astype(jnp.bfloat16), v_ref[0, pl.ds(k0, tq), lanes[h]])
            carry = carry + (later_h[:, 0:1] + sps[h][:, 0:1])
            new_state.append((carry, acc))
        return tuple(new_state)

    init = tuple((jnp.zeros((tq, 1), jnp.float32), jnp.zeros((tq, HEAD_DIM), jnp.float32))
                 for _ in range(n_heads))
    state = block(i, init, True)
    state = lax.fori_loop(0, i, lambda it, st: block(i - 1 - it, st, False), state)
    for h in range(n_heads):
        out_ref[0, :, lanes[h]] = state[h][1].astype(out_ref.dtype)


def _sb(proj):
    b, t, _ = proj.shape
    tq = TQ_SB
    hw = SB_GROUP_WIDTH
    pairs = SB_WIDTH // hw
    q0, k0, v0 = COL_SB // hw, (COL_SB + SB_WIDTH) // hw, (COL_SB + 2 * SB_WIDTH) // hw
    return pl.pallas_call(
        _sb_kernel,
        grid=(b, pairs, t // tq),
        in_specs=[
            pl.BlockSpec((1, tq, hw), lambda bi, hp, i: (bi, i, q0 + hp)),
            pl.BlockSpec((1, t, hw), lambda bi, hp, i: (bi, 0, k0 + hp)),
            pl.BlockSpec((1, t, hw), lambda bi, hp, i: (bi, 0, v0 + hp)),
        ],
        out_specs=pl.BlockSpec((1, tq, hw), lambda bi, hp, i: (bi, i, hp)),
        out_shape=jax.ShapeDtypeStruct((b, t, SB_WIDTH), jnp.bfloat16),
        compiler_params=pltpu.CompilerParams(
            dimension_semantics=("parallel", "parallel", "arbitrary"), vmem_limit_bytes=VMEM_LIMIT),
        name="sb",
    )(proj, proj, proj)


def _mix_kernel(x_ref, on_ref, os_ref, gpre_ref, wg_ref, wpn_ref, wps_ref, wo_ref, gpost_ref, out_ref):
    x = x_ref[...]
    h = _rms_norm(x, gpre_ref[...]).astype(jnp.bfloat16)
    gm = jax.nn.sigmoid(_dot(h, wg_ref[...]))
    y = gm[:, :D_MODEL] * _dot(on_ref[...], wpn_ref[...]) + gm[:, D_MODEL:] * _dot(os_ref[...], wps_ref[...])
    m = _dot(y.astype(jnp.bfloat16), wo_ref[...])
    out_ref[...] = x + _rms_norm(m, gpost_ref[...])


def _mix(x2d, o_nsa, o_sb, g_pre, w_gm, w_pn, w_ps, w_out, g_post):
    n = x2d.shape[0]
    tm = TM_MIX
    const = lambda i: (0, 0)
    row = lambda i: (i, 0)
    return pl.pallas_call(
        _mix_kernel,
        grid=(n // tm,),
        in_specs=[
            pl.BlockSpec((tm, D_MODEL), row),
            pl.BlockSpec((tm, NSA_WIDTH), row),
            pl.BlockSpec((tm, SB_WIDTH), row),
            pl.BlockSpec((1, D_MODEL), const),
            pl.BlockSpec(w_gm.shape, const),
            pl.BlockSpec(w_pn.shape, const),
            pl.BlockSpec(w_ps.shape, const),
            pl.BlockSpec(w_out.shape, const),
            pl.BlockSpec((1, D_MODEL), const),
        ],
        out_specs=pl.BlockSpec((tm, D_MODEL), row),
        out_shape=jax.ShapeDtypeStruct((n, D_MODEL), jnp.float32),
        compiler_params=pltpu.CompilerParams(
            dimension_semantics=("parallel",), vmem_limit_bytes=VMEM_LIMIT),
        name="mix",
    )(x2d, o_nsa, o_sb, g_pre, w_gm, w_pn, w_ps, w_out, g_post)


def _ffn_kernel(x_ref, gpre_ref, wup_ref, cw_ref, cb_ref, wdn_ref, gpost_ref, out_ref, halo_ref):
    tm = TM_FFN
    ti = pl.program_id(1)

    @pl.when(ti == 0)
    def _():
        halo_ref[...] = jnp.zeros_like(halo_ref)

    x = x_ref[0]
    h = _rms_norm(x, gpre_ref[...]).astype(jnp.bfloat16)
    row = lax.broadcasted_iota(jnp.int32, (tm, 1), 0)

    def conv(u, c0):
        cols = slice(c0, c0 + FF_CHUNK)
        prev = halo_ref[:, cols]
        halo_ref[:, cols] = u[tm - HALO:]
        u1 = jnp.where(row == 0, prev[HALO - 1:HALO], pltpu.roll(u, 1, 0))
        u2 = jnp.where(row == 0, prev[HALO - 2:HALO - 1],
                       jnp.where(row == 1, prev[HALO - 1:HALO], pltpu.roll(u, 2, 0)))
        return cw_ref[0:1, cols] * u2 + cw_ref[1:2, cols] * u1 + cw_ref[2:3, cols] * u + cb_ref[:, cols]

    f = jnp.zeros((tm, D_MODEL), jnp.float32)
    for c in range(D_FF // FF_CHUNK):
        g0 = c * FF_CHUNK
        u0 = D_FF + c * FF_CHUNK
        gate = conv(_dot(h, wup_ref[:, g0:g0 + FF_CHUNK]), g0)
        up = conv(_dot(h, wup_ref[:, u0:u0 + FF_CHUNK]), u0)
        act = (jax.nn.gelu(gate, approximate=True) * up).astype(jnp.bfloat16)
        f = f + _dot(act, wdn_ref[g0:g0 + FF_CHUNK, :])
    out_ref[0] = x + _rms_norm(f, gpost_ref[...])


def _ffn(x, g_pre, w_up, conv_w, conv_b, w_down, g_post):
    b, t, _ = x.shape
    tm = TM_FFN
    const = lambda bi, i: (0, 0)
    return pl.pallas_call(
        _ffn_kernel,
        grid=(b, t // tm),
        in_specs=[
            pl.BlockSpec((1, tm, D_MODEL), lambda bi, i: (bi, i, 0)),
            pl.BlockSpec((1, D_MODEL), const),
            pl.BlockSpec(w_up.shape, const),
            pl.BlockSpec(conv_w.shape, const),
            pl.BlockSpec(conv_b.shape, const),
            pl.BlockSpec(w_down.shape, const),
            pl.BlockSpec((1, D_MODEL), const),
        ],
        out_specs=pl.BlockSpec((1, tm, D_MODEL), lambda bi, i: (bi, i, 0)),
        out_shape=jax.ShapeDtypeStruct(x.shape, jnp.float32),
        scratch_shapes=[pltpu.VMEM((HALO, 2 * D_FF), jnp.float32)],
        compiler_params=pltpu.CompilerParams(
            dimension_semantics=("parallel", "arbitrary"), vmem_limit_bytes=VMEM_LIMIT),
        name="ffn",
    )(x, g_pre, w_up, conv_w, conv_b, w_down, g_post)


def _overlap_matrix_t(seq_len, n_cols):
    n_cmp = (seq_len - CMP_BLOCK) // CMP_STRIDE + 1
    n_slc = seq_len // SLC_BLOCK
    ci = np.arange(n_cmp)[None, :] * CMP_STRIDE
    sj = np.arange(n_slc)[:, None] * SLC_BLOCK
    ovl = np.zeros((n_slc, n_cols), np.float32)
    ovl[:, :n_cmp] = ((ci < sj + SLC_BLOCK) & (sj < ci + CMP_BLOCK)).astype(np.float32)
    return jnp.asarray(ovl)


def kernel(x, positions, g_pre_mix, w_in, ck_pe, ck_w1, ck_b1, ck_w2, cv_pe, cv_w1, cv_b1, cv_w2,
           w_proj_nsa, w_proj_sb, w_out, g_post_mix, g_pre_ffn, w_up, conv_w, conv_b, w_down, g_post_ffn):
    b, t, d = x.shape
    depth = w_in.shape[0]
    assert d == D_MODEL and t % TM_FFN == 0 and t % TQ_SB == 0 and t >= WINDOW + TQ_NSA
    assert t % (CMP_STRIDE * 8) == 0 and t // SLC_BLOCK <= LANES
    bf = jnp.bfloat16
    n_chunk = t // CMP_STRIDE

    half = HEAD_DIM // 2
    inv_freq = ROPE_THETA ** (-jnp.arange(half, dtype=jnp.float32) / half)
    freq = jnp.tile(inv_freq, LANES // half)[None, :]
    sign = jnp.asarray(np.tile(np.concatenate([-np.ones(half), np.ones(half)]), LANES // HEAD_DIM),
                       jnp.float32)[None, :]
    overlap_t = _overlap_matrix_t(t, n_chunk)
    pos2d = positions.reshape(b * t, 1)

    c_gate = NSA_WIDTH + 6 * KV_WIDTH
    c_sb = c_gate + N_GATE
    c_merge = c_sb + 3 * SB_WIDTH

    for l in range(depth):
        wl = w_in[l]
        w_main = jnp.concatenate(
            [wl[:, c_sb:c_merge], wl[:, :c_gate], wl[:, c_gate:c_sb],
             jnp.zeros((d, LANES - N_GATE), wl.dtype)], axis=1).astype(bf)
        w_gm = wl[:, c_merge:].astype(bf)

        proj, gates = _in_proj(x.reshape(b * t, d), pos2d, g_pre_mix[l][None, :], w_main, freq, sign)
        proj = proj.reshape(b, t, PROJ_WIDTH)
        gates = gates.reshape(b, t, LANES)

        chunks = proj[:, :, COL_KV:COL_KV + 2 * KV_WIDTH].reshape(b, t, 2 * NSA_KV_HEADS, HEAD_DIM)
        chunks = jnp.transpose(chunks, (0, 2, 1, 3)).reshape(b, 2 * NSA_KV_HEADS, n_chunk, CMP_STRIDE * HEAD_DIM)
        cmp_kv = _compress(
            chunks,
            jnp.stack([ck_pe[l].reshape(1, -1), cv_pe[l].reshape(1, -1)]),
            jnp.stack([ck_w1[l], cv_w1[l]]).astype(bf),
            jnp.stack([ck_b1[l][None, :], cv_b1[l][None, :]]),
            jnp.stack([ck_w2[l], cv_w2[l]]).astype(bf))

        o_nsa = _nsa(proj, cmp_kv, gates, overlap_t)
        o_sb = _sb(proj)

        x = _mix(x.reshape(b * t, d), o_nsa.reshape(b * t, NSA_WIDTH), o_sb.reshape(b * t, SB_WIDTH),
                 g_pre_mix[l][None, :], w_gm, w_proj_nsa[l].astype(bf), w_proj_sb[l].astype(bf),
                 w_out[l].astype(bf), g_post_mix[l][None, :]).reshape(b, t, d)
        x = _ffn(x, g_pre_ffn[l][None, :], w_up[l].astype(bf), conv_w[l], conv_b[l][None, :],
                 w_down[l].astype(bf), g_post_ffn[l][None, :])
    return x
```

```python
import functools

import numpy as np
import jax
import jax.numpy as jnp
from jax import lax
from jax.experimental import pallas as pl
from jax.experimental.pallas import tpu as pltpu

D_MODEL = 1024
HEAD_DIM = 64
NSA_HEADS = 8
NSA_KV_HEADS = 2
NSA_GROUP = NSA_HEADS // NSA_KV_HEADS
SB_HEADS = 8
CMP_BLOCK = 32
CMP_STRIDE = 16
SLC_BLOCK = 64
SLC_TOPK = 16
WINDOW = 512
D_FF = 2816
ROPE_THETA = 10000.0
RMS_EPS = 1e-6
NEG_INF = -1e30
FORCE_BONUS = 1e4
LOG2_E = 1.4426950408889634

NSA_WIDTH = NSA_HEADS * HEAD_DIM
KV_WIDTH = NSA_KV_HEADS * HEAD_DIM
SB_WIDTH = SB_HEADS * HEAD_DIM
N_GATE = 3 * NSA_HEADS

LANES = 128
VMEM_LIMIT = 56 * 1024 * 1024

COL_SB = 0
COL_Q = COL_SB + 3 * SB_WIDTH
COL_KV = COL_Q + NSA_WIDTH
PROJ_WIDTH = COL_KV + 6 * KV_WIDTH
W_MAIN_WIDTH = PROJ_WIDTH + LANES

TM_IN = 512
TQ_NSA = 128
TK_SLC = 256
TQ_SB = 256
SB_GROUP_WIDTH = 512
TM_MIX = 512
TM_FFN = 512
FF_CHUNK = 256
HALO = 8

assert COL_SB % SB_GROUP_WIDTH == 0 and SB_WIDTH % SB_GROUP_WIDTH == 0
assert COL_Q % NSA_WIDTH == 0 and COL_KV % (2 * KV_WIDTH) == 0


def _dot(a, b):
    return jnp.dot(a, b, preferred_element_type=jnp.float32)


def _dot_nt(a, b):
    return lax.dot_general(a, b, (((1,), (1,)), ((), ())), preferred_element_type=jnp.float32)


def _dot_tn(a, b):
    return lax.dot_general(a, b, (((0,), (0,)), ((), ())), preferred_element_type=jnp.float32)


def _rms_norm(xf, gain):
    y = xf * lax.rsqrt(jnp.mean(xf * xf, axis=-1, keepdims=True) + RMS_EPS)
    return y * gain


def _rope_blocks(y, cos, sin_signed, n_blocks, rope_block):
    lane = lax.broadcasted_iota(jnp.int32, (1, LANES), 1)
    first_half = (lane % HEAD_DIM) < (HEAD_DIM // 2)
    outs = []
    for b in range(n_blocks):
        blk = y[:, b * LANES:(b + 1) * LANES]
        if rope_block(b):
            partner = jnp.where(first_half,
                                pltpu.roll(blk, LANES - HEAD_DIM // 2, 1),
                                pltpu.roll(blk, HEAD_DIM // 2, 1))
            blk = blk * cos + partner * sin_signed
        outs.append(blk)
    return outs


def _in_proj_kernel(x_ref, pos_ref, g_ref, w_ref, freq_ref, sign_ref, proj_ref, gate_ref):
    scale = HEAD_DIM ** -0.5
    h = _rms_norm(x_ref[...], g_ref[...]).astype(jnp.bfloat16)
    ang = pos_ref[...].astype(jnp.float32) * freq_ref[...]
    cos = jnp.cos(ang)
    sin_signed = jnp.sin(ang) * sign_ref[...]

    yq = _dot(h, w_ref[:, COL_Q:COL_Q + NSA_WIDTH])
    for b, blk in enumerate(_rope_blocks(yq, cos, sin_signed, NSA_WIDTH // LANES, lambda b: True)):
        proj_ref[:, COL_Q + b * LANES:COL_Q + (b + 1) * LANES] = (blk * (scale * LOG2_E)).astype(proj_ref.dtype)

    ykv = _dot(h, w_ref[:, COL_KV:COL_KV + 6 * KV_WIDTH])
    for b, blk in enumerate(_rope_blocks(ykv, cos, sin_signed, 6, lambda b: b % 2 == 0)):
        proj_ref[:, COL_KV + b * LANES:COL_KV + (b + 1) * LANES] = blk.astype(proj_ref.dtype)

    ysb = _dot(h, w_ref[:, COL_SB:COL_SB + 3 * SB_WIDTH])
    proj_ref[:, COL_SB:COL_SB + SB_WIDTH] = (ysb[:, :SB_WIDTH] * scale).astype(proj_ref.dtype)
    proj_ref[:, COL_SB + SB_WIDTH:COL_SB + 3 * SB_WIDTH] = ysb[:, SB_WIDTH:].astype(proj_ref.dtype)

    gate_ref[...] = jax.nn.sigmoid(_dot(h, w_ref[:, PROJ_WIDTH:W_MAIN_WIDTH]))


def _in_proj(x2d, pos2d, gain, w_main, freq, sign):
    n = x2d.shape[0]
    tm = TM_IN
    const = lambda i: (0, 0)
    return pl.pallas_call(
        _in_proj_kernel,
        grid=(n // tm,),
        in_specs=[
            pl.BlockSpec((tm, D_MODEL), lambda i: (i, 0)),
            pl.BlockSpec((tm, 1), lambda i: (i, 0)),
            pl.BlockSpec((1, D_MODEL), const),
            pl.BlockSpec((D_MODEL, W_MAIN_WIDTH), const),
            pl.BlockSpec((1, LANES), const),
            pl.BlockSpec((1, LANES), const),
        ],
        out_specs=[
            pl.BlockSpec((tm, PROJ_WIDTH), lambda i: (i, 0)),
            pl.BlockSpec((tm, LANES), lambda i: (i, 0)),
        ],
        out_shape=[
            jax.ShapeDtypeStruct((n, PROJ_WIDTH), jnp.bfloat16),
            jax.ShapeDtypeStruct((n, LANES), jnp.float32),
        ],
        compiler_params=pltpu.CompilerParams(
            dimension_semantics=("parallel",), vmem_limit_bytes=VMEM_LIMIT),
        name="in_proj",
    )(x2d, pos2d, gain, w_main, freq, sign)


def _compress_kernel(x_ref, pe_ref, w1_ref, b1_ref, w2_ref, out_ref):
    half = CMP_STRIDE * HEAD_DIM
    for j in range(2 * NSA_KV_HEADS):
        p = j // NSA_KV_HEADS
        x = x_ref[0, j]
        w1 = w1_ref[p]
        top = _dot(x, w1[:half])
        bot = _dot(x, w1[half:])
        n_chunk = x.shape[0]
        bot_next = pltpu.roll(bot, n_chunk - 1, 0)
        pe_row = jnp.broadcast_to(pe_ref[p], (8, 2 * half)).astype(jnp.bfloat16)
        bias = _dot(pe_row, w1)[0:1] + b1_ref[p]
        hid = jax.nn.gelu(top + bot_next + bias, approximate=True)
        out_ref[0, j] = _dot(hid.astype(jnp.bfloat16), w2_ref[p])


def _compress(chunks, pe, w1, b1, w2):
    b, four, n_chunk, width = chunks.shape
    c3 = lambda i: (0, 0, 0)
    return pl.pallas_call(
        _compress_kernel,
        grid=(b,),
        in_specs=[
            pl.BlockSpec((1, four, n_chunk, width), lambda i: (i, 0, 0, 0)),
            pl.BlockSpec(pe.shape, c3),
            pl.BlockSpec(w1.shape, c3),
            pl.BlockSpec(b1.shape, c3),
            pl.BlockSpec(w2.shape, c3),
        ],
        out_specs=pl.BlockSpec((1, four, n_chunk, HEAD_DIM), lambda i: (i, 0, 0, 0)),
        out_shape=jax.ShapeDtypeStruct((b, four, n_chunk, HEAD_DIM), jnp.float32),
        compiler_params=pltpu.CompilerParams(
            dimension_semantics=("parallel",), vmem_limit_bytes=VMEM_LIMIT),
        name="compress",
    )(chunks, pe, w1, b1, w2)


def _nsa_kernel(q_ref, kvs_ref, kvw_ref, cmp_ref, gate_ref, ovlt_ref, out_ref, *, seq_len):
    tq = TQ_NSA
    g = NSA_GROUP
    heads = range(NSA_KV_HEADS)
    i = pl.program_id(1)
    t0 = i * tq
    n_cmp_pad = cmp_ref.shape[2]
    n_cmp = (seq_len - CMP_BLOCK) // CMP_STRIDE + 1
    n_slc = seq_len // SLC_BLOCK
    top_n = min(SLC_TOPK, n_slc)
    span = WINDOW + tq
    slc_shift = SLC_BLOCK.bit_length() - 1

    t_row = t0 + lax.broadcasted_iota(jnp.int32, (1, tq), 1)

    def k_lanes(hk):
        return slice(hk * HEAD_DIM, (hk + 1) * HEAD_DIM)

    def v_lanes(hk):
        return slice(KV_WIDTH + hk * HEAD_DIM, KV_WIDTH + (hk + 1) * HEAD_DIM)

    def tile_lanes(a):
        return jnp.concatenate([a] * g, axis=1)

    q_st = [jnp.concatenate(
        [q_ref[0, :, (hk * g + gi) * HEAD_DIM:(hk * g + gi + 1) * HEAD_DIM] for gi in range(g)], axis=0)
        for hk in heads]

    n_col = lax.broadcasted_iota(jnp.int32, (n_cmp_pad, 1), 0)
    cmask = tile_lanes((n_col * CMP_STRIDE + (CMP_BLOCK - 1) <= t_row) & (n_col < n_cmp))
    s_c = [_dot_nt(cmp_ref[0, hk].astype(jnp.bfloat16), q_st[hk]) for hk in heads]
    p_c = []
    for hk in heads:
        m = jnp.max(jnp.where(cmask, s_c[hk], NEG_INF), axis=0, keepdims=True)
        e = jnp.where(cmask, jnp.exp2(s_c[hk] - m), 0.0)
        den = jnp.sum(e, axis=0, keepdims=True)
        p_c.append(e / jnp.where(den > 0.0, den, 1.0))
    o_cmp = [_dot_tn(cmp_ref[0, NSA_KV_HEADS + hk].astype(jnp.bfloat16), p_c[hk].astype(jnp.bfloat16))
             for hk in heads]

    blk = lax.broadcasted_iota(jnp.int32, (n_slc, 1), 0)
    cur = jnp.right_shift(t_row, slc_shift)
    valid = blk * SLC_BLOCK <= t_row
    bonus = jnp.where((blk == 0) | (blk == cur) | (blk == cur - 1), FORCE_BONUS, 0.0)
    sel = []
    for hk in heads:
        p_sum = p_c[hk][:, 0:tq]
        for gi in range(1, g):
            p_sum = p_sum + p_c[hk][:, gi * tq:(gi + 1) * tq]
        p_slc = jnp.dot(ovlt_ref[...], p_sum, precision=lax.Precision.HIGHEST,
                        preferred_element_type=jnp.float32)
        score = jnp.where(valid, p_slc + bonus, -1.0)
        rank = jnp.zeros((n_slc, tq), jnp.float32)
        for c in range(n_slc):
            r = score[c:c + 1, :]
            rank = rank + ((r > score) | ((r >= score) & (blk > c))).astype(jnp.float32)
        sel.append(((rank < top_n) & (score >= 0.0)).astype(jnp.bfloat16))

    def slc_chunk(c, carry, causal):
        k0 = pl.multiple_of(c * TK_SLC, TK_SLC)
        tok = k0 + lax.broadcasted_iota(jnp.int32, (TK_SLC, 1), 0)
        expand = (jnp.right_shift(tok, slc_shift)
                  == lax.broadcasted_iota(jnp.int32, (TK_SLC, n_slc), 1)).astype(jnp.bfloat16)
        biases = []
        for hk in heads:
            keep = _dot(expand, sel[hk]) > 0.5
            if causal:
                keep = keep & (tok <= t_row)
            biases.append(tile_lanes(jnp.where(keep, 0.0, NEG_INF)))
        ss = [_dot_nt(kvs_ref[0, pl.ds(k0, TK_SLC), k_lanes(hk)], q_st[hk]) + biases[hk]
              for hk in heads]
        ps, stats = [], []
        for hk in heads:
            m_prev, l_prev, _ = carry[hk]
            m_new = jnp.maximum(m_prev, jnp.max(ss[hk], axis=0, keepdims=True))
            alpha = jnp.exp2(m_prev - m_new)
            p = jnp.exp2(ss[hk] - m_new)
            stats.append((m_new, alpha * l_prev + jnp.sum(p, axis=0, keepdims=True), alpha))
            ps.append(p.astype(jnp.bfloat16))
        new = []
        for hk in heads:
            m_new, l_new, alpha = stats[hk]
            pv = _dot_tn(kvs_ref[0, pl.ds(k0, TK_SLC), v_lanes(hk)], ps[hk])
            new.append((m_new, l_new, alpha * carry[hk][2] + pv))
        return tuple(new)

    last = (t0 + tq - 1) // TK_SLC
    init = tuple((jnp.full((1, g * tq), NEG_INF, jnp.float32),
                  jnp.zeros((1, g * tq), jnp.float32),
                  jnp.zeros((HEAD_DIM, g * tq), jnp.float32)) for _ in heads)
    carry = lax.fori_loop(0, last, lambda c, st: slc_chunk(c, st, False), init)
    carry = slc_chunk(last, carry, True)
    o_slc = [carry[hk][2] / carry[hk][1] for hk in heads]

    start = pl.multiple_of(jnp.maximum(t0 - WINDOW, 0), tq)
    diff = t_row - (start + lax.broadcasted_iota(jnp.int32, (span, 1), 0))
    wbias = tile_lanes(jnp.where((diff >= 0) & (diff < WINDOW), 0.0, NEG_INF))
    s_w = [_dot_nt(kvw_ref[0, pl.ds(start, span), k_lanes(hk)], q_st[hk]) + wbias for hk in heads]
    e_w = [jnp.exp2(s - jnp.max(s, axis=0, keepdims=True)) for s in s_w]
    o_win = [_dot_tn(kvw_ref[0, pl.ds(start, span), v_lanes(hk)], e_w[hk].astype(jnp.bfloat16))
             / jnp.sum(e_w[hk], axis=0, keepdims=True) for hk in heads]

    gates = gate_ref[0].T
    for hk in heads:
        for gi in range(g):
            hd = hk * g + gi
            cols = slice(gi * tq, (gi + 1) * tq)
            o = (gates[3 * hd:3 * hd + 1] * o_cmp[hk][:, cols]
                 + gates[3 * hd + 1:3 * hd + 2] * o_slc[hk][:, cols]
                 + gates[3 * hd + 2:3 * hd + 3] * o_win[hk][:, cols])
            out_ref[0, :, hd * HEAD_DIM:(hd + 1) * HEAD_DIM] = o.T.astype(out_ref.dtype)


def _nsa(proj, cmp_kv, gates, overlap_t):
    b, t, _ = proj.shape
    tq = TQ_NSA
    pair = 2 * KV_WIDTH
    return pl.pallas_call(
        functools.partial(_nsa_kernel, seq_len=t),
        grid=(b, t // tq),
        in_specs=[
            pl.BlockSpec((1, tq, NSA_WIDTH), lambda bi, i: (bi, i, COL_Q // NSA_WIDTH)),
            pl.BlockSpec((1, t, pair), lambda bi, i: (bi, 0, (COL_KV + pair) // pair)),
            pl.BlockSpec((1, t, pair), lambda bi, i: (bi, 0, (COL_KV + 2 * pair) // pair)),
            pl.BlockSpec((1,) + cmp_kv.shape[1:], lambda bi, i: (bi, 0, 0, 0)),
            pl.BlockSpec((1, tq, LANES), lambda bi, i: (bi, i, 0)),
            pl.BlockSpec(overlap_t.shape, lambda bi, i: (0, 0)),
        ],
        out_specs=pl.BlockSpec((1, tq, NSA_WIDTH), lambda bi, i: (bi, i, 0)),
        out_shape=jax.ShapeDtypeStruct((b, t, NSA_WIDTH), jnp.bfloat16),
        compiler_params=pltpu.CompilerParams(
            dimension_semantics=("parallel", "arbitrary"), vmem_limit_bytes=VMEM_LIMIT),
        name="nsa",
    )(proj, proj, proj, cmp_kv, gates, overlap_t)


def _sb_kernel(q_ref, k_ref, v_ref, out_ref):
    tq = TQ_SB
    i = pl.program_id(2)
    n_heads = q_ref.shape[2] // HEAD_DIM
    row = lax.broadcasted_iota(jnp.int32, (tq, tq), 0)
    col = lax.broadcasted_iota(jnp.int32, (tq, tq), 1)
    upper = (row > col).astype(jnp.bfloat16)
    strict = col < row
    lanes = [slice(h * HEAD_DIM, (h + 1) * HEAD_DIM) for h in range(n_heads)]
    qs = [q_ref[0, :, ln] for ln in lanes]

    def block(kb, state, diagonal):
        k0 = pl.multiple_of(kb * tq, tq)
        zs = [_dot_nt(qs[h], k_ref[0, pl.ds(k0, tq), lanes[h]]) for h in range(n_heads)]
        sps, logits = [], []
        for z in zs:
            neg_abs = lax.bitcast_convert_type(
                lax.bitcast_convert_type(z, jnp.uint32) | jnp.uint32(0x80000000), jnp.float32)
            sp = jnp.maximum(z, 0.0) + jnp.log(1.0 + jnp.exp(neg_abs))
            if diagonal:
                sp = jnp.where(strict, sp, 0.0)
            sps.append(sp)
            logits.append(z - sp)
        later = _dot(jnp.concatenate([sp.astype(jnp.bfloat16) for sp in sps], axis=0), upper)
        new_state = []
        for h in range(n_heads):
            carry, acc = state[h]
            later_h = later[h * tq:(h + 1) * tq]
            w = jnp.exp(logits[h] - later_h)
            if diagonal:
                w = jnp.where(strict, w, 0.0)
            pv = _dot(w.astype(jnp.bfloat16), v_ref[0, pl.ds(k0, tq), lanes[h]])
            acc = pv if diagonal else acc + jnp.exp(-carry) * pv
            carry = carry + (later_h[:, 0:1] + sps[h][:, 0:1])
            new_state.append((carry, acc))
        return tuple(new_state)

    init = tuple((jnp.zeros((tq, 1), jnp.float32), jnp.zeros((tq, HEAD_DIM), jnp.float32))
                 for _ in range(n_heads))
    state = block(i, init, True)
    state = lax.fori_loop(0, i, lambda it, st: block(i - 1 - it, st, False), state)
    for h in range(n_heads):
        out_ref[0, :, lanes[h]] = state[h][1].astype(out_ref.dtype)


def _sb(proj):
    b, t, _ = proj.shape
    tq = TQ_SB
    hw = SB_GROUP_WIDTH
    pairs = SB_WIDTH // hw
    q0, k0, v0 = COL_SB // hw, (COL_SB + SB_WIDTH) // hw, (COL_SB + 2 * SB_WIDTH) // hw
    return pl.pallas_call(
        _sb_kernel,
        grid=(b, pairs, t // tq),
        in_specs=[
            pl.BlockSpec((1, tq, hw), lambda bi, hp, i: (bi, i, q0 + hp)),
            pl.BlockSpec((1, t, hw), lambda bi, hp, i: (bi, 0, k0 + hp)),
            pl.BlockSpec((1, t, hw), lambda bi, hp, i: (bi, 0, v0 + hp)),
        ],
        out_specs=pl.BlockSpec((1, tq, hw), lambda bi, hp, i: (bi, i, hp)),
        out_shape=jax.ShapeDtypeStruct((b, t, SB_WIDTH), jnp.bfloat16),
        compiler_params=pltpu.CompilerParams(
            dimension_semantics=("parallel", "parallel", "arbitrary"), vmem_limit_bytes=VMEM_LIMIT),
        name="sb",
    )(proj, proj, proj)


def _mix_kernel(x_ref, on_ref, os_ref, gpre_ref, wg_ref, wpn_ref, wps_ref, wo_ref, gpost_ref, out_ref):
    x = x_ref[...]
    h = _rms_norm(x, gpre_ref[...]).astype(jnp.bfloat16)
    gm = jax.nn.sigmoid(_dot(h, wg_ref[...]))
    y = gm[:, :D_MODEL] * _dot(on_ref[...], wpn_ref[...]) + gm[:, D_MODEL:] * _dot(os_ref[...], wps_ref[...])
    m = _dot(y.astype(jnp.bfloat16), wo_ref[...])
    out_ref[...] = x + _rms_norm(m, gpost_ref[...])


def _mix(x2d, o_nsa, o_sb, g_pre, w_gm, w_pn, w_ps, w_out, g_post):
    n = x2d.shape[0]
    tm = TM_MIX
    const = lambda i: (0, 0)
    row = lambda i: (i, 0)
    return pl.pallas_call(
        _mix_kernel,
        grid=(n // tm,),
        in_specs=[
            pl.BlockSpec((tm, D_MODEL), row),
            pl.BlockSpec((tm, NSA_WIDTH), row),
            pl.BlockSpec((tm, SB_WIDTH), row),
            pl.BlockSpec((1, D_MODEL), const),
            pl.BlockSpec(w_gm.shape, const),
            pl.BlockSpec(w_pn.shape, const),
            pl.BlockSpec(w_ps.shape, const),
            pl.BlockSpec(w_out.shape, const),
            pl.BlockSpec((1, D_MODEL), const),
        ],
        out_specs=pl.BlockSpec((tm, D_MODEL), row),
        out_shape=jax.ShapeDtypeStruct((n, D_MODEL), jnp.float32),
        compiler_params=pltpu.CompilerParams(
            dimension_semantics=("parallel",), vmem_limit_bytes=VMEM_LIMIT),
        name="mix",
    )(x2d, o_nsa, o_sb, g_pre, w_gm, w_pn, w_ps, w_out, g_post)


def _ffn_kernel(x_ref, gpre_ref, wup_ref, cw_ref, cb_ref, wdn_ref, gpost_ref, out_ref, halo_ref):
    tm = TM_FFN
    ti = pl.program_id(1)

    @pl.when(ti == 0)
    def _():
        halo_ref[...] = jnp.zeros_like(halo_ref)

    x = x_ref[0]
    h = _rms_norm(x, gpre_ref[...]).astype(jnp.bfloat16)
    row = lax.broadcasted_iota(jnp.int32, (HALO, 1), 0)

    def conv(u, c0):
        cols = slice(c0, c0 + FF_CHUNK)
        prev = halo_ref[:, cols]
        halo_ref[:, cols] = u[tm - HALO:]
        r1 = pltpu.roll(u, 1, 0)
        r2 = pltpu.roll(u, 2, 0)
        head1 = jnp.where(row == 0, prev[HALO - 1:HALO], r1[:HALO])
        head2 = jnp.where(row == 0, prev[HALO - 2:HALO - 1], jnp.where(row == 1, prev[HALO - 1:HALO], r2[:HALO]))
        u1 = jnp.concatenate([head1, r1[HALO:]], axis=0)
        u2 = jnp.concatenate([head2, r2[HALO:]], axis=0)
        return cw_ref[0:1, cols] * u2 + cw_ref[1:2, cols] * u1 + cw_ref[2:3, cols] * u + cb_ref[:, cols]

    def up_dots(c):
        g0 = c * FF_CHUNK
        u0 = D_FF + c * FF_CHUNK
        return _dot(h, wup_ref[:, g0:g0 + FF_CHUNK]), _dot(h, wup_ref[:, u0:u0 + FF_CHUNK])

    n_chunks = D_FF // FF_CHUNK
    f = jnp.zeros((tm, D_MODEL), jnp.float32)
    ug, uu = up_dots(0)
    for c in range(n_chunks):
        nxt = up_dots(c + 1) if c + 1 < n_chunks else None
        g0 = c * FF_CHUNK
        gate = conv(ug, g0)
        up = conv(uu, D_FF + g0)
        act = (jax.nn.gelu(gate, approximate=True) * up).astype(jnp.bfloat16)
        f = f + _dot(act, wdn_ref[g0:g0 + FF_CHUNK, :])
        if nxt is not None:
            ug, uu = nxt
    out_ref[0] = x + _rms_norm(f, gpost_ref[...])


def _ffn(x, g_pre, w_up, conv_w, conv_b, w_down, g_post):
    b, t, _ = x.shape
    tm = TM_FFN
    const = lambda bi, i: (0, 0)
    return pl.pallas_call(
        _ffn_kernel,
        grid=(b, t // tm),
        in_specs=[
            pl.BlockSpec((1, tm, D_MODEL), lambda bi, i: (bi, i, 0)),
            pl.BlockSpec((1, D_MODEL), const),
            pl.BlockSpec(w_up.shape, const),
            pl.BlockSpec(conv_w.shape, const),
            pl.BlockSpec(conv_b.shape, const),
            pl.BlockSpec(w_down.shape, const),
            pl.BlockSpec((1, D_MODEL), const),
        ],
        out_specs=pl.BlockSpec((1, tm, D_MODEL), lambda bi, i: (bi, i, 0)),
        out_shape=jax.ShapeDtypeStruct(x.shape, jnp.float32),
        scratch_shapes=[pltpu.VMEM((HALO, 2 * D_FF), jnp.float32)],
        compiler_params=pltpu.CompilerParams(
            dimension_semantics=("parallel", "arbitrary"), vmem_limit_bytes=VMEM_LIMIT),
        name="ffn",
    )(x, g_pre, w_up, conv_w, conv_b, w_down, g_post)


def _overlap_matrix_t(seq_len, n_cols):
    n_cmp = (seq_len - CMP_BLOCK) // CMP_STRIDE + 1
    n_slc = seq_len // SLC_BLOCK
    ci = np.arange(n_cmp)[None, :] * CMP_STRIDE
    sj = np.arange(n_slc)[:, None] * SLC_BLOCK
    ovl = np.zeros((n_slc, n_cols), np.float32)
    ovl[:, :n_cmp] = ((ci < sj + SLC_BLOCK) & (sj < ci + CMP_BLOCK)).astype(np.float32)
    return jnp.asarray(ovl)


def kernel(x, positions, g_pre_mix, w_in, ck_pe, ck_w1, ck_b1, ck_w2, cv_pe, cv_w1, cv_b1, cv_w2,
           w_proj_nsa, w_proj_sb, w_out, g_post_mix, g_pre_ffn, w_up, conv_w, conv_b, w_down, g_post_ffn):
    b, t, d = x.shape
    depth = w_in.shape[0]
    assert d == D_MODEL and t % TM_FFN == 0 and t % TQ_SB == 0 and t >= WINDOW + TQ_NSA
    assert t % (CMP_STRIDE * 8) == 0 and t // SLC_BLOCK <= LANES
    bf = jnp.bfloat16
    n_chunk = t // CMP_STRIDE

    half = HEAD_DIM // 2
    inv_freq = ROPE_THETA ** (-jnp.arange(half, dtype=jnp.float32) / half)
    freq = jnp.tile(inv_freq, LANES // half)[None, :]
    sign = jnp.asarray(np.tile(np.concatenate([-np.ones(half), np.ones(half)]), LANES // HEAD_DIM),
                       jnp.float32)[None, :]
    overlap_t = _overlap_matrix_t(t, n_chunk)
    pos2d = positions.reshape(b * t, 1)

    c_gate = NSA_WIDTH + 6 * KV_WIDTH
    c_sb = c_gate + N_GATE
    c_merge = c_sb + 3 * SB_WIDTH

    for l in range(depth):
        wl = w_in[l]
        w_main = jnp.concatenate(
            [wl[:, c_sb:c_merge], wl[:, :c_gate], wl[:, c_gate:c_sb],
             jnp.zeros((d, LANES - N_GATE), wl.dtype)], axis=1).astype(bf)
        w_gm = wl[:, c_merge:].astype(bf)

        proj, gates = _in_proj(x.reshape(b * t, d), pos2d, g_pre_mix[l][None, :], w_main, freq, sign)
        proj = proj.reshape(b, t, PROJ_WIDTH)
        gates = gates.reshape(b, t, LANES)

        chunks = proj[:, :, COL_KV:COL_KV + 2 * KV_WIDTH].reshape(b, t, 2 * NSA_KV_HEADS, HEAD_DIM)
        chunks = jnp.transpose(chunks, (0, 2, 1, 3)).reshape(b, 2 * NSA_KV_HEADS, n_chunk, CMP_STRIDE * HEAD_DIM)
        cmp_kv = _compress(
            chunks,
            jnp.stack([ck_pe[l].reshape(1, -1), cv_pe[l].reshape(1, -1)]),
            jnp.stack([ck_w1[l], cv_w1[l]]).astype(bf),
            jnp.stack([ck_b1[l][None, :], cv_b1[l][None, :]]),
            jnp.stack([ck_w2[l], cv_w2[l]]).astype(bf))

        o_nsa = _nsa(proj, cmp_kv, gates, overlap_t)
        o_sb = _sb(proj)

        x = _mix(x.reshape(b * t, d), o_nsa.reshape(b * t, NSA_WIDTH), o_sb.reshape(b * t, SB_WIDTH),
                 g_pre_mix[l][None, :], w_gm, w_proj_nsa[l].astype(bf), w_proj_sb[l].astype(bf),
                 w_out[l].astype(bf), g_post_mix[l][None, :]).reshape(b, t, d)
        x = _ffn(x, g_pre_ffn[l][None, :], w_up[l].astype(bf), conv_w[l], conv_b[l][None, :],
                 w_down[l].astype(bf), g_post_ffn[l][None, :])
    return x
```

```python
import functools

import numpy as np
import jax
import jax.numpy as jnp
from jax import lax
from jax.experimental import pallas as pl
from jax.experimental.pallas import tpu as pltpu

D_MODEL = 1024
HEAD_DIM = 64
NSA_HEADS = 8
NSA_KV_HEADS = 2
NSA_GROUP = NSA_HEADS // NSA_KV_HEADS
SB_HEADS = 8
CMP_BLOCK = 32
CMP_STRIDE = 16
SLC_BLOCK = 64
SLC_TOPK = 16
WINDOW = 512
D_FF = 2816
ROPE_THETA = 10000.0
RMS_EPS = 1e-6
NEG_INF = -1e30
FORCE_BONUS = 1e4
LOG2_E = 1.4426950408889634

NSA_WIDTH = NSA_HEADS * HEAD_DIM
KV_WIDTH = NSA_KV_HEADS * HEAD_DIM
SB_WIDTH = SB_HEADS * HEAD_DIM
N_GATE = 3 * NSA_HEADS

LANES = 128
VMEM_LIMIT = 56 * 1024 * 1024

COL_SB = 0
COL_Q = COL_SB + 3 * SB_WIDTH
COL_KV = COL_Q + NSA_WIDTH
PROJ_WIDTH = COL_KV + 4 * KV_WIDTH
CMP_IN_WIDTH = 2 * KV_WIDTH
W_MAIN_WIDTH = PROJ_WIDTH + CMP_IN_WIDTH + LANES

TM_IN = 512
TQ_NSA = 128
TK_SLC = 256
TQ_SB = 256
SB_GROUP_WIDTH = 512
TM_MIX = 512
TM_FFN = 512
FF_CHUNK = 256
HALO = 8

assert COL_SB % SB_GROUP_WIDTH == 0 and SB_WIDTH % SB_GROUP_WIDTH == 0
assert COL_Q % NSA_WIDTH == 0 and COL_KV % (2 * KV_WIDTH) == 0


def _dot(a, b):
    return jnp.dot(a, b, preferred_element_type=jnp.float32)


def _dot_nt(a, b):
    return lax.dot_general(a, b, (((1,), (1,)), ((), ())), preferred_element_type=jnp.float32)


def _dot_tn(a, b):
    return lax.dot_general(a, b, (((0,), (0,)), ((), ())), preferred_element_type=jnp.float32)


def _rms_norm(xf, gain):
    y = xf * lax.rsqrt(jnp.mean(xf * xf, axis=-1, keepdims=True) + RMS_EPS)
    return y * gain


def _rope_blocks(y, cos, sin_signed, n_blocks, rope_block):
    lane = lax.broadcasted_iota(jnp.int32, (1, LANES), 1)
    first_half = (lane % HEAD_DIM) < (HEAD_DIM // 2)
    outs = []
    for b in range(n_blocks):
        blk = y[:, b * LANES:(b + 1) * LANES]
        if rope_block(b):
            partner = jnp.where(first_half,
                                pltpu.roll(blk, LANES - HEAD_DIM // 2, 1),
                                pltpu.roll(blk, HEAD_DIM // 2, 1))
            blk = blk * cos + partner * sin_signed
        outs.append(blk)
    return outs


def _in_proj_kernel(x_ref, pos_ref, g_ref, w_ref, freq_ref, sign_ref, proj_ref, cmp_ref, gate_ref):
    scale = HEAD_DIM ** -0.5
    h = _rms_norm(x_ref[...], g_ref[...]).astype(jnp.bfloat16)
    ang = pos_ref[...].astype(jnp.float32) * freq_ref[...]
    cos = jnp.cos(ang)
    sin_signed = jnp.sin(ang) * sign_ref[...]

    yq = _dot(h, w_ref[:, COL_Q:COL_Q + NSA_WIDTH])
    for b, blk in enumerate(_rope_blocks(yq, cos, sin_signed, NSA_WIDTH // LANES, lambda b: True)):
        proj_ref[:, COL_Q + b * LANES:COL_Q + (b + 1) * LANES] = (blk * (scale * LOG2_E)).astype(proj_ref.dtype)

    ykv = _dot(h, w_ref[:, COL_KV:COL_KV + 6 * KV_WIDTH])
    for b, blk in enumerate(_rope_blocks(ykv, cos, sin_signed, 6, lambda b: b % 2 == 0)):
        if b < 4:
            proj_ref[:, COL_KV + b * LANES:COL_KV + (b + 1) * LANES] = blk.astype(proj_ref.dtype)
        else:
            cmp_ref[:, (b - 4) * LANES:(b - 3) * LANES] = blk.astype(cmp_ref.dtype)

    ysb = _dot(h, w_ref[:, COL_SB:COL_SB + 3 * SB_WIDTH])
    proj_ref[:, COL_SB:COL_SB + SB_WIDTH] = (ysb[:, :SB_WIDTH] * scale).astype(proj_ref.dtype)
    proj_ref[:, COL_SB + SB_WIDTH:COL_SB + 3 * SB_WIDTH] = ysb[:, SB_WIDTH:].astype(proj_ref.dtype)

    gate_ref[...] = jax.nn.sigmoid(_dot(h, w_ref[:, W_MAIN_WIDTH - LANES:W_MAIN_WIDTH]))


def _in_proj(x2d, pos2d, gain, w_main, freq, sign):
    n = x2d.shape[0]
    tm = TM_IN
    const = lambda i: (0, 0)
    return pl.pallas_call(
        _in_proj_kernel,
        grid=(n // tm,),
        in_specs=[
            pl.BlockSpec((tm, D_MODEL), lambda i: (i, 0)),
            pl.BlockSpec((tm, 1), lambda i: (i, 0)),
            pl.BlockSpec((1, D_MODEL), const),
            pl.BlockSpec((D_MODEL, W_MAIN_WIDTH), const),
            pl.BlockSpec((1, LANES), const),
            pl.BlockSpec((1, LANES), const),
        ],
        out_specs=[
            pl.BlockSpec((tm, PROJ_WIDTH), lambda i: (i, 0)),
            pl.BlockSpec((tm, CMP_IN_WIDTH), lambda i: (i, 0)),
            pl.BlockSpec((tm, LANES), lambda i: (i, 0)),
        ],
        out_shape=[
            jax.ShapeDtypeStruct((n, PROJ_WIDTH), jnp.bfloat16),
            jax.ShapeDtypeStruct((n, CMP_IN_WIDTH), jnp.bfloat16),
            jax.ShapeDtypeStruct((n, LANES), jnp.float32),
        ],
        compiler_params=pltpu.CompilerParams(
            dimension_semantics=("parallel",), vmem_limit_bytes=VMEM_LIMIT),
        name="in_proj",
    )(x2d, pos2d, gain, w_main, freq, sign)


def _compress_kernel(x_ref, pe_ref, w1_ref, b1_ref, w2_ref, out_ref):
    x = x_ref[0]
    n_chunk = x.shape[0]
    top = _dot(x, w1_ref[0])
    bot = _dot(x, w1_ref[1])
    bot_next = pltpu.roll(bot, n_chunk - 1, 0)
    pe = jnp.broadcast_to(pe_ref[...], (2, 8, x.shape[1])).astype(jnp.bfloat16)
    bias = _dot(pe[0], w1_ref[0])[0:1] + _dot(pe[1], w1_ref[1])[0:1] + b1_ref[...]
    hid = jax.nn.gelu(top + bot_next + bias, approximate=True)
    out_ref[0] = _dot(hid.astype(jnp.bfloat16), w2_ref[...])


def _compress(x, pe, w1, b1, w2):
    b, n_chunk, width = x.shape
    out_w = w2.shape[1]
    return pl.pallas_call(
        _compress_kernel,
        grid=(b,),
        in_specs=[
            pl.BlockSpec((1, n_chunk, width), lambda i: (i, 0, 0)),
            pl.BlockSpec(pe.shape, lambda i: (0, 0, 0)),
            pl.BlockSpec(w1.shape, lambda i: (0, 0, 0)),
            pl.BlockSpec(b1.shape, lambda i: (0, 0)),
            pl.BlockSpec(w2.shape, lambda i: (0, 0)),
        ],
        out_specs=pl.BlockSpec((1, n_chunk, out_w), lambda i: (i, 0, 0)),
        out_shape=jax.ShapeDtypeStruct((b, n_chunk, out_w), jnp.float32),
        compiler_params=pltpu.CompilerParams(
            dimension_semantics=("parallel",), vmem_limit_bytes=VMEM_LIMIT),
        name="compress",
    )(x, pe, w1, b1, w2)


def _compress_params(k_pe, k_w1, k_b1, k_w2, v_pe, v_w1, v_b1, v_w2):
    hd, stride = HEAD_DIM, CMP_STRIDE
    n_str = 2 * NSA_KV_HEADS
    eye = jnp.eye(n_str, dtype=k_w1.dtype)

    def per_stream(k, v):
        return jnp.stack([k] * NSA_KV_HEADS + [v] * NSA_KV_HEADS)

    w1 = per_stream(k_w1, v_w1).reshape(n_str, 2, stride, hd, hd)
    w1 = jnp.einsum('jk,jhlde->hljdke', eye, w1).reshape(2, stride * n_str * hd, n_str * hd)
    pe = per_stream(k_pe, v_pe).reshape(n_str, 2, stride, hd)
    pe = jnp.transpose(pe, (1, 2, 0, 3)).reshape(2, 1, stride * n_str * hd)
    b1 = per_stream(k_b1, v_b1).reshape(1, n_str * hd)
    w2 = jnp.einsum('jk,jde->jdke', eye, per_stream(k_w2, v_w2)).reshape(n_str * hd, n_str * hd)
    return pe, w1.astype(jnp.bfloat16), b1, w2.astype(jnp.bfloat16)


def _nsa_kernel(q_ref, kvs_ref, kvw_ref, cmp_ref, gate_ref, ovlt_ref, out_ref, *, seq_len):
    tq = TQ_NSA
    g = NSA_GROUP
    heads = range(NSA_KV_HEADS)
    i = pl.program_id(1)
    t0 = i * tq
    n_cmp_pad = cmp_ref.shape[1]
    n_cmp = (seq_len - CMP_BLOCK) // CMP_STRIDE + 1
    n_slc = seq_len // SLC_BLOCK
    top_n = min(SLC_TOPK, n_slc)
    span = WINDOW + tq
    slc_shift = SLC_BLOCK.bit_length() - 1

    t_row = t0 + lax.broadcasted_iota(jnp.int32, (1, tq), 1)

    def k_lanes(hk):
        return slice(hk * HEAD_DIM, (hk + 1) * HEAD_DIM)

    def v_lanes(hk):
        return slice(KV_WIDTH + hk * HEAD_DIM, KV_WIDTH + (hk + 1) * HEAD_DIM)

    def tile_lanes(a):
        return jnp.concatenate([a] * g, axis=1)

    q_st = [jnp.concatenate(
        [q_ref[0, :, (hk * g + gi) * HEAD_DIM:(hk * g + gi + 1) * HEAD_DIM] for gi in range(g)], axis=0)
        for hk in heads]

    n_col = lax.broadcasted_iota(jnp.int32, (n_cmp_pad, 1), 0)
    cmask = tile_lanes((n_col * CMP_STRIDE + (CMP_BLOCK - 1) <= t_row) & (n_col < n_cmp))
    s_c = [_dot_nt(cmp_ref[0, :, k_lanes(hk)].astype(jnp.bfloat16), q_st[hk]) for hk in heads]
    p_c = []
    for hk in heads:
        m = jnp.max(jnp.where(cmask, s_c[hk], NEG_INF), axis=0, keepdims=True)
        e = jnp.where(cmask, jnp.exp2(s_c[hk] - m), 0.0)
        den = jnp.sum(e, axis=0, keepdims=True)
        p_c.append(e / jnp.where(den > 0.0, den, 1.0))
    o_cmp = [_dot_tn(cmp_ref[0, :, v_lanes(hk)].astype(jnp.bfloat16), p_c[hk].astype(jnp.bfloat16))
             for hk in heads]

    blk = lax.broadcasted_iota(jnp.int32, (n_slc, 1), 0)
    cur = jnp.right_shift(t_row, slc_shift)
    valid = blk * SLC_BLOCK <= t_row
    bonus = jnp.where((blk == 0) | (blk == cur) | (blk == cur - 1), FORCE_BONUS, 0.0)
    sel = []
    for hk in heads:
        p_sum = p_c[hk][:, 0:tq]
        for gi in range(1, g):
            p_sum = p_sum + p_c[hk][:, gi * tq:(gi + 1) * tq]
        p_slc = jnp.dot(ovlt_ref[...], p_sum, precision=lax.Precision.HIGHEST,
                        preferred_element_type=jnp.float32)
        score = jnp.where(valid, p_slc + bonus, -1.0)
        rank = jnp.zeros((n_slc, tq), jnp.float32)
        for c in range(n_slc):
            r = score[c:c + 1, :]
            rank = rank + ((r > score) | ((r >= score) & (blk > c))).astype(jnp.float32)
        sel.append(((rank < top_n) & (score >= 0.0)).astype(jnp.bfloat16))

    def slc_chunk(c, carry, causal):
        k0 = pl.multiple_of(c * TK_SLC, TK_SLC)
        tok = k0 + lax.broadcasted_iota(jnp.int32, (TK_SLC, 1), 0)
        expand = (jnp.right_shift(tok, slc_shift)
                  == lax.broadcasted_iota(jnp.int32, (TK_SLC, n_slc), 1)).astype(jnp.bfloat16)
        biases = []
        for hk in heads:
            keep = _dot(expand, sel[hk]) > 0.5
            if causal:
                keep = keep & (tok <= t_row)
            biases.append(tile_lanes(jnp.where(keep, 0.0, NEG_INF)))
        ss = [_dot_nt(kvs_ref[0, pl.ds(k0, TK_SLC), k_lanes(hk)], q_st[hk]) + biases[hk]
              for hk in heads]
        ps, stats = [], []
        for hk in heads:
            m_prev, l_prev, _ = carry[hk]
            m_new = jnp.maximum(m_prev, jnp.max(ss[hk], axis=0, keepdims=True))
            alpha = jnp.exp2(m_prev - m_new)
            p = jnp.exp2(ss[hk] - m_new)
            stats.append((m_new, alpha * l_prev + jnp.sum(p, axis=0, keepdims=True), alpha))
            ps.append(p.astype(jnp.bfloat16))
        new = []
        for hk in heads:
            m_new, l_new, alpha = stats[hk]
            pv = _dot_tn(kvs_ref[0, pl.ds(k0, TK_SLC), v_lanes(hk)], ps[hk])
            new.append((m_new, l_new, alpha * carry[hk][2] + pv))
        return tuple(new)

    last = (t0 + tq - 1) // TK_SLC
    init = tuple((jnp.full((1, g * tq), NEG_INF, jnp.float32),
                  jnp.zeros((1, g * tq), jnp.float32),
                  jnp.zeros((HEAD_DIM, g * tq), jnp.float32)) for _ in heads)
    carry = lax.fori_loop(0, last, lambda c, st: slc_chunk(c, st, False), init)
    carry = slc_chunk(last, carry, True)
    o_slc = [carry[hk][2] / carry[hk][1] for hk in heads]

    start = pl.multiple_of(jnp.maximum(t0 - WINDOW, 0), tq)
    diff = t_row - (start + lax.broadcasted_iota(jnp.int32, (span, 1), 0))
    wbias = tile_lanes(jnp.where((diff >= 0) & (diff < WINDOW), 0.0, NEG_INF))
    s_w = [_dot_nt(kvw_ref[0, pl.ds(start, span), k_lanes(hk)], q_st[hk]) + wbias for hk in heads]
    e_w = [jnp.exp2(s - jnp.max(s, axis=0, keepdims=True)) for s in s_w]
    o_win = [_dot_tn(kvw_ref[0, pl.ds(start, span), v_lanes(hk)], e_w[hk].astype(jnp.bfloat16))
             / jnp.sum(e_w[hk], axis=0, keepdims=True) for hk in heads]

    gates = gate_ref[0].T
    for hk in heads:
        for gi in range(g):
            hd = hk * g + gi
            cols = slice(gi * tq, (gi + 1) * tq)
            o = (gates[3 * hd:3 * hd + 1] * o_cmp[hk][:, cols]
                 + gates[3 * hd + 1:3 * hd + 2] * o_slc[hk][:, cols]
                 + gates[3 * hd + 2:3 * hd + 3] * o_win[hk][:, cols])
            out_ref[0, :, hd * HEAD_DIM:(hd + 1) * HEAD_DIM] = o.T.astype(out_ref.dtype)


def _nsa(proj, cmp_kv, gates, overlap_t):
    b, t, _ = proj.shape
    tq = TQ_NSA
    pair = 2 * KV_WIDTH
    return pl.pallas_call(
        functools.partial(_nsa_kernel, seq_len=t),
        grid=(b, t // tq),
        in_specs=[
            pl.BlockSpec((1, tq, NSA_WIDTH), lambda bi, i: (bi, i, COL_Q // NSA_WIDTH)),
            pl.BlockSpec((1, t, pair), lambda bi, i: (bi, 0, COL_KV // pair)),
            pl.BlockSpec((1, t, pair), lambda bi, i: (bi, 0, (COL_KV + pair) // pair)),
            pl.BlockSpec((1,) + cmp_kv.shape[1:], lambda bi, i: (bi, 0, 0)),
            pl.BlockSpec((1, tq, LANES), lambda bi, i: (bi, i, 0)),
            pl.BlockSpec(overlap_t.shape, lambda bi, i: (0, 0)),
        ],
        out_specs=pl.BlockSpec((1, tq, NSA_WIDTH), lambda bi, i: (bi, i, 0)),
        out_shape=jax.ShapeDtypeStruct((b, t, NSA_WIDTH), jnp.bfloat16),
        compiler_params=pltpu.CompilerParams(
            dimension_semantics=("parallel", "arbitrary"), vmem_limit_bytes=VMEM_LIMIT),
        name="nsa",
    )(proj, proj, proj, cmp_kv, gates, overlap_t)


def _sb_kernel(q_ref, k_ref, v_ref, out_ref):
    tq = TQ_SB
    i = pl.program_id(2)
    n_heads = q_ref.shape[2] // HEAD_DIM
    row = lax.broadcasted_iota(jnp.int32, (tq, tq), 0)
    col = lax.broadcasted_iota(jnp.int32, (tq, tq), 1)
    lower = (col > row).astype(jnp.bfloat16)
    strict = row < col
    lanes = [slice(h * HEAD_DIM, (h + 1) * HEAD_DIM) for h in range(n_heads)]
    qs = [q_ref[0, :, ln] for ln in lanes]

    def block(kb, state, diagonal):
        k0 = pl.multiple_of(kb * tq, tq)
        zs = [_dot_nt(k_ref[0, pl.ds(k0, tq), lanes[h]], qs[h]) for h in range(n_heads)]
        sps, logits = [], []
        for z in zs:
            neg_abs = lax.bitcast_convert_type(
                lax.bitcast_convert_type(z, jnp.uint32) | jnp.uint32(0x80000000), jnp.float32)
            sp = jnp.maximum(z, 0.0) + jnp.log(1.0 + jnp.exp(neg_abs))
            if diagonal:
                sp = jnp.where(strict, sp, 0.0)
            sps.append(sp)
            logits.append(z - sp)
        later = _dot(lower, jnp.concatenate([sp.astype(jnp.bfloat16) for sp in sps], axis=1))
        new_state = []
        for h in range(n_heads):
            carry, acc = state[h]
            later_h = later[:, h * tq:(h + 1) * tq]
            w = jnp.exp(logits[h] - later_h)
            if diagonal:
                w = jnp.where(strict, w, 0.0)
            pv = _dot_tn(v_ref[0, pl.ds(k0, tq), lanes[h]], w.astype(jnp.bfloat16))
            acc = pv if diagonal else acc + jnp.exp(-carry) * pv
            carry = carry + (later_h[0:1, :] + sps[h][0:1, :])
            new_state.append((carry, acc))
        return tuple(new_state)

    init = tuple((jnp.zeros((1, tq), jnp.float32), jnp.zeros((HEAD_DIM, tq), jnp.float32))
                 for _ in range(n_heads))
    state = block(i, init, True)
    state = lax.fori_loop(0, i, lambda it, st: block(i - 1 - it, st, False), state)
    for h in range(n_heads):
        out_ref[0, :, lanes[h]] = state[h][1].T.astype(out_ref.dtype)


def _sb(proj):
    b, t, _ = proj.shape
    tq = TQ_SB
    hw = SB_GROUP_WIDTH
    pairs = SB_WIDTH // hw
    q0, k0, v0 = COL_SB // hw, (COL_SB + SB_WIDTH) // hw, (COL_SB + 2 * SB_WIDTH) // hw
    return pl.pallas_call(
        _sb_kernel,
        grid=(b, pairs, t // tq),
        in_specs=[
            pl.BlockSpec((1, tq, hw), lambda bi, hp, i: (bi, i, q0 + hp)),
            pl.BlockSpec((1, t, hw), lambda bi, hp, i: (bi, 0, k0 + hp)),
            pl.BlockSpec((1, t, hw), lambda bi, hp, i: (bi, 0, v0 + hp)),
        ],
        out_specs=pl.BlockSpec((1, tq, hw), lambda bi, hp, i: (bi, i, hp)),
        out_shape=jax.ShapeDtypeStruct((b, t, SB_WIDTH), jnp.bfloat16),
        compiler_params=pltpu.CompilerParams(
            dimension_semantics=("parallel", "parallel", "arbitrary"), vmem_limit_bytes=VMEM_LIMIT),
        name="sb",
    )(proj, proj, proj)


def _mix_kernel(x_ref, on_ref, os_ref, gpre_ref, wg_ref, wpn_ref, wps_ref, wo_ref, gpost_ref, out_ref):
    x = x_ref[...]
    h = _rms_norm(x, gpre_ref[...]).astype(jnp.bfloat16)
    gm = jax.nn.sigmoid(_dot(h, wg_ref[...]))
    y = gm[:, :D_MODEL] * _dot(on_ref[...], wpn_ref[...]) + gm[:, D_MODEL:] * _dot(os_ref[...], wps_ref[...])
    m = _dot(y.astype(jnp.bfloat16), wo_ref[...])
    out_ref[...] = x + _rms_norm(m, gpost_ref[...])


def _mix(x2d, o_nsa, o_sb, g_pre, w_gm, w_pn, w_ps, w_out, g_post):
    n = x2d.shape[0]
    tm = TM_MIX
    const = lambda i: (0, 0)
    row = lambda i: (i, 0)
    return pl.pallas_call(
        _mix_kernel,
        grid=(n // tm,),
        in_specs=[
            pl.BlockSpec((tm, D_MODEL), row),
            pl.BlockSpec((tm, NSA_WIDTH), row),
            pl.BlockSpec((tm, SB_WIDTH), row),
            pl.BlockSpec((1, D_MODEL), const),
            pl.BlockSpec(w_gm.shape, const),
            pl.BlockSpec(w_pn.shape, const),
            pl.BlockSpec(w_ps.shape, const),
            pl.BlockSpec(w_out.shape, const),
            pl.BlockSpec((1, D_MODEL), const),
        ],
        out_specs=pl.BlockSpec((tm, D_MODEL), row),
        out_shape=jax.ShapeDtypeStruct((n, D_MODEL), jnp.float32),
        compiler_params=pltpu.CompilerParams(
            dimension_semantics=("parallel",), vmem_limit_bytes=VMEM_LIMIT),
        name="mix",
    )(x2d, o_nsa, o_sb, g_pre, w_gm, w_pn, w_ps, w_out, g_post)


def _ffn_kernel(x_ref, gpre_ref, wup_ref, cw_ref, cb_ref, wdn_ref, gpost_ref, out_ref, halo_ref):
    tm = TM_FFN
    ti = pl.program_id(1)

    @pl.when(ti == 0)
    def _():
        halo_ref[...] = jnp.zeros_like(halo_ref)

    x = x_ref[0]
    h = _rms_norm(x, gpre_ref[...]).astype(jnp.bfloat16)
    row = lax.broadcasted_iota(jnp.int32, (HALO, 1), 0)

    def conv(u, c0):
        cols = slice(c0, c0 + FF_CHUNK)
        prev = halo_ref[:, cols]
        halo_ref[:, cols] = u[tm - HALO:]
        r1 = pltpu.roll(u, 1, 0)
        r2 = pltpu.roll(u, 2, 0)
        head1 = jnp.where(row == 0, prev[HALO - 1:HALO], r1[:HALO])
        head2 = jnp.where(row == 0, prev[HALO - 2:HALO - 1], jnp.where(row == 1, prev[HALO - 1:HALO], r2[:HALO]))
        u1 = jnp.concatenate([head1, r1[HALO:]], axis=0)
        u2 = jnp.concatenate([head2, r2[HALO:]], axis=0)
        return cw_ref[0:1, cols] * u2 + cw_ref[1:2, cols] * u1 + cw_ref[2:3, cols] * u + cb_ref[:, cols]

    def up_dots(c):
        g0 = c * FF_CHUNK
        u0 = D_FF + c * FF_CHUNK
        return _dot(h, wup_ref[:, g0:g0 + FF_CHUNK]), _dot(h, wup_ref[:, u0:u0 + FF_CHUNK])

    n_chunks = D_FF // FF_CHUNK
    f = jnp.zeros((tm, D_MODEL), jnp.float32)
    ug, uu = up_dots(0)
    for c in range(n_chunks):
        nxt = up_dots(c + 1) if c + 1 < n_chunks else None
        g0 = c * FF_CHUNK
        gate = conv(ug, g0)
        up = conv(uu, D_FF + g0)
        act = (jax.nn.gelu(gate, approximate=True) * up).astype(jnp.bfloat16)
        f = f + _dot(act, wdn_ref[g0:g0 + FF_CHUNK, :])
        if nxt is not None:
            ug, uu = nxt
    out_ref[0] = x + _rms_norm(f, gpost_ref[...])


def _ffn(x, g_pre, w_up, conv_w, conv_b, w_down, g_post):
    b, t, _ = x.shape
    tm = TM_FFN
    const = lambda bi, i: (0, 0)
    return pl.pallas_call(
        _ffn_kernel,
        grid=(b, t // tm),
        in_specs=[
            pl.BlockSpec((1, tm, D_MODEL), lambda bi, i: (bi, i, 0)),
            pl.BlockSpec((1, D_MODEL), const),
            pl.BlockSpec(w_up.shape, const),
            pl.BlockSpec(conv_w.shape, const),
            pl.BlockSpec(conv_b.shape, const),
            pl.BlockSpec(w_down.shape, const),
            pl.BlockSpec((1, D_MODEL), const),
        ],
        out_specs=pl.BlockSpec((1, tm, D_MODEL), lambda bi, i: (bi, i, 0)),
        out_shape=jax.ShapeDtypeStruct(x.shape, jnp.float32),
        scratch_shapes=[pltpu.VMEM((HALO, 2 * D_FF), jnp.float32)],
        compiler_params=pltpu.CompilerParams(
            dimension_semantics=("parallel", "arbitrary"), vmem_limit_bytes=VMEM_LIMIT),
        name="ffn",
    )(x, g_pre, w_up, conv_w, conv_b, w_down, g_post)


def _overlap_matrix_t(seq_len, n_cols):
    n_cmp = (seq_len - CMP_BLOCK) // CMP_STRIDE + 1
    n_slc = seq_len // SLC_BLOCK
    ci = np.arange(n_cmp)[None, :] * CMP_STRIDE
    sj = np.arange(n_slc)[:, None] * SLC_BLOCK
    ovl = np.zeros((n_slc, n_cols), np.float32)
    ovl[:, :n_cmp] = ((ci < sj + SLC_BLOCK) & (sj < ci + CMP_BLOCK)).astype(np.float32)
    return jnp.asarray(ovl)


def kernel(x, positions, g_pre_mix, w_in, ck_pe, ck_w1, ck_b1, ck_w2, cv_pe, cv_w1, cv_b1, cv_w2,
           w_proj_nsa, w_proj_sb, w_out, g_post_mix, g_pre_ffn, w_up, conv_w, conv_b, w_down, g_post_ffn):
    b, t, d = x.shape
    depth = w_in.shape[0]
    assert d == D_MODEL and t % TM_FFN == 0 and t % TQ_SB == 0 and t >= WINDOW + TQ_NSA
    assert t % (CMP_STRIDE * 8) == 0 and t // SLC_BLOCK <= LANES
    bf = jnp.bfloat16
    n_chunk = t // CMP_STRIDE

    half = HEAD_DIM // 2
    inv_freq = ROPE_THETA ** (-jnp.arange(half, dtype=jnp.float32) / half)
    freq = jnp.tile(inv_freq, LANES // half)[None, :]
    sign = jnp.asarray(np.tile(np.concatenate([-np.ones(half), np.ones(half)]), LANES // HEAD_DIM),
                       jnp.float32)[None, :]
    overlap_t = _overlap_matrix_t(t, n_chunk)
    pos2d = positions.reshape(b * t, 1)

    c_gate = NSA_WIDTH + 6 * KV_WIDTH
    c_sb = c_gate + N_GATE
    c_merge = c_sb + 3 * SB_WIDTH

    for l in range(depth):
        wl = w_in[l]
        c_cmp = NSA_WIDTH + CMP_IN_WIDTH
        w_main = jnp.concatenate(
            [wl[:, c_sb:c_merge], wl[:, :NSA_WIDTH], wl[:, c_cmp:c_gate], wl[:, NSA_WIDTH:c_cmp],
             wl[:, c_gate:c_sb], jnp.zeros((d, LANES - N_GATE), wl.dtype)], axis=1).astype(bf)
        w_gm = wl[:, c_merge:].astype(bf)

        proj, cmp_in, gates = _in_proj(x.reshape(b * t, d), pos2d, g_pre_mix[l][None, :], w_main, freq, sign)
        proj = proj.reshape(b, t, PROJ_WIDTH)
        gates = gates.reshape(b, t, LANES)

        cmp_kv = _compress(
            cmp_in.reshape(b, n_chunk, CMP_STRIDE * CMP_IN_WIDTH),
            *_compress_params(ck_pe[l], ck_w1[l], ck_b1[l], ck_w2[l], cv_pe[l], cv_w1[l], cv_b1[l], cv_w2[l]))

        o_nsa = _nsa(proj, cmp_kv, gates, overlap_t)
        o_sb = _sb(proj)

        x = _mix(x.reshape(b * t, d), o_nsa.reshape(b * t, NSA_WIDTH), o_sb.reshape(b * t, SB_WIDTH),
                 g_pre_mix[l][None, :], w_gm, w_proj_nsa[l].astype(bf), w_proj_sb[l].astype(bf),
                 w_out[l].astype(bf), g_post_mix[l][None, :]).reshape(b, t, d)
        x = _ffn(x, g_pre_ffn[l][None, :], w_up[l].astype(bf), conv_w[l], conv_b[l][None, :],
                 w_down[l].astype(bf), g_post_ffn[l][None, :])
    return x
```

```python
import functools

import numpy as np
import jax
import jax.numpy as jnp
from jax import lax
from jax.experimental import pallas as pl
from jax.experimental.pallas import tpu as pltpu

D_MODEL = 1024
HEAD_DIM = 64
NSA_HEADS = 8
NSA_KV_HEADS = 2
NSA_GROUP = NSA_HEADS // NSA_KV_HEADS
SB_HEADS = 8
CMP_BLOCK = 32
CMP_STRIDE = 16
SLC_BLOCK = 64
SLC_TOPK = 16
WINDOW = 512
D_FF = 2816
ROPE_THETA = 10000.0
RMS_EPS = 1e-6
NEG_INF = -1e30
FORCE_BONUS = 1e4
LOG2_E = 1.4426950408889634

NSA_WIDTH = NSA_HEADS * HEAD_DIM
KV_WIDTH = NSA_KV_HEADS * HEAD_DIM
SB_WIDTH = SB_HEADS * HEAD_DIM
N_GATE = 3 * NSA_HEADS

LANES = 128
VMEM_LIMIT = 56 * 1024 * 1024

COL_SB = 0
COL_Q = COL_SB + 3 * SB_WIDTH
COL_KV = COL_Q + NSA_WIDTH
PROJ_WIDTH = COL_KV + 4 * KV_WIDTH
CMP_IN_WIDTH = 2 * KV_WIDTH
W_MAIN_WIDTH = PROJ_WIDTH + CMP_IN_WIDTH + LANES

TM_IN = 512
TQ_NSA = 256
TK_SLC = 256
TQ_SB = 256
SB_GROUP_WIDTH = 512
TM_MIX = 512
TM_FFN = 512
FF_CHUNK = 256
HALO = 8

assert COL_SB % SB_GROUP_WIDTH == 0 and SB_WIDTH % SB_GROUP_WIDTH == 0
assert COL_Q % NSA_WIDTH == 0 and COL_KV % (2 * KV_WIDTH) == 0


def _dot(a, b):
    return jnp.dot(a, b, preferred_element_type=jnp.float32)


def _dot_nt(a, b):
    return lax.dot_general(a, b, (((1,), (1,)), ((), ())), preferred_element_type=jnp.float32)


def _dot_tn(a, b):
    return lax.dot_general(a, b, (((0,), (0,)), ((), ())), preferred_element_type=jnp.float32)


def _rms_norm(xf, gain):
    y = xf * lax.rsqrt(jnp.mean(xf * xf, axis=-1, keepdims=True) + RMS_EPS)
    return y * gain


def _rope_blocks(y, cos, sin_signed, n_blocks, rope_block):
    lane = lax.broadcasted_iota(jnp.int32, (1, LANES), 1)
    first_half = (lane % HEAD_DIM) < (HEAD_DIM // 2)
    outs = []
    for b in range(n_blocks):
        blk = y[:, b * LANES:(b + 1) * LANES]
        if rope_block(b):
            partner = jnp.where(first_half,
                                pltpu.roll(blk, LANES - HEAD_DIM // 2, 1),
                                pltpu.roll(blk, HEAD_DIM // 2, 1))
            blk = blk * cos + partner * sin_signed
        outs.append(blk)
    return outs


def _in_proj_kernel(x_ref, pos_ref, g_ref, w_ref, freq_ref, sign_ref, proj_ref, cmp_ref, gate_ref):
    scale = HEAD_DIM ** -0.5
    h = _rms_norm(x_ref[...], g_ref[...]).astype(jnp.bfloat16)
    ang = pos_ref[...].astype(jnp.float32) * freq_ref[...]
    cos = jnp.cos(ang)
    sin_signed = jnp.sin(ang) * sign_ref[...]

    yq = _dot(h, w_ref[:, COL_Q:COL_Q + NSA_WIDTH])
    for b, blk in enumerate(_rope_blocks(yq, cos, sin_signed, NSA_WIDTH // LANES, lambda b: True)):
        proj_ref[:, COL_Q + b * LANES:COL_Q + (b + 1) * LANES] = (blk * (scale * LOG2_E)).astype(proj_ref.dtype)

    ykv = _dot(h, w_ref[:, COL_KV:COL_KV + 6 * KV_WIDTH])
    for b, blk in enumerate(_rope_blocks(ykv, cos, sin_signed, 6, lambda b: b % 2 == 0)):
        if b < 4:
            proj_ref[:, COL_KV + b * LANES:COL_KV + (b + 1) * LANES] = blk.astype(proj_ref.dtype)
        else:
            cmp_ref[:, (b - 4) * LANES:(b - 3) * LANES] = blk.astype(cmp_ref.dtype)

    ysb = _dot(h, w_ref[:, COL_SB:COL_SB + 3 * SB_WIDTH])
    proj_ref[:, COL_SB:COL_SB + SB_WIDTH] = (ysb[:, :SB_WIDTH] * scale).astype(proj_ref.dtype)
    proj_ref[:, COL_SB + SB_WIDTH:COL_SB + 3 * SB_WIDTH] = ysb[:, SB_WIDTH:].astype(proj_ref.dtype)

    gate_ref[...] = jax.nn.sigmoid(_dot(h, w_ref[:, W_MAIN_WIDTH - LANES:W_MAIN_WIDTH]))


def _in_proj(x2d, pos2d, gain, w_main, freq, sign):
    n = x2d.shape[0]
    tm = TM_IN
    const = lambda i: (0, 0)
    return pl.pallas_call(
        _in_proj_kernel,
        grid=(n // tm,),
        in_specs=[
            pl.BlockSpec((tm, D_MODEL), lambda i: (i, 0)),
            pl.BlockSpec((tm, 1), lambda i: (i, 0)),
            pl.BlockSpec((1, D_MODEL), const),
            pl.BlockSpec((D_MODEL, W_MAIN_WIDTH), const),
            pl.BlockSpec((1, LANES), const),
            pl.BlockSpec((1, LANES), const),
        ],
        out_specs=[
            pl.BlockSpec((tm, PROJ_WIDTH), lambda i: (i, 0)),
            pl.BlockSpec((tm, CMP_IN_WIDTH), lambda i: (i, 0)),
            pl.BlockSpec((tm, LANES), lambda i: (i, 0)),
        ],
        out_shape=[
            jax.ShapeDtypeStruct((n, PROJ_WIDTH), jnp.bfloat16),
            jax.ShapeDtypeStruct((n, CMP_IN_WIDTH), jnp.bfloat16),
            jax.ShapeDtypeStruct((n, LANES), jnp.float32),
        ],
        compiler_params=pltpu.CompilerParams(
            dimension_semantics=("parallel",), vmem_limit_bytes=VMEM_LIMIT),
        name="in_proj",
    )(x2d, pos2d, gain, w_main, freq, sign)


def _compress_kernel(x_ref, pe_ref, w1_ref, b1_ref, w2_ref, out_ref):
    x = x_ref[0]
    n_chunk = x.shape[0]
    top = _dot(x, w1_ref[0])
    bot = _dot(x, w1_ref[1])
    bot_next = pltpu.roll(bot, n_chunk - 1, 0)
    pe = jnp.broadcast_to(pe_ref[...], (2, 8, x.shape[1])).astype(jnp.bfloat16)
    bias = _dot(pe[0], w1_ref[0])[0:1] + _dot(pe[1], w1_ref[1])[0:1] + b1_ref[...]
    hid = jax.nn.gelu(top + bot_next + bias, approximate=True)
    out_ref[0] = _dot(hid.astype(jnp.bfloat16), w2_ref[...])


def _compress(x, pe, w1, b1, w2):
    b, n_chunk, width = x.shape
    out_w = w2.shape[1]
    return pl.pallas_call(
        _compress_kernel,
        grid=(b,),
        in_specs=[
            pl.BlockSpec((1, n_chunk, width), lambda i: (i, 0, 0)),
            pl.BlockSpec(pe.shape, lambda i: (0, 0, 0)),
            pl.BlockSpec(w1.shape, lambda i: (0, 0, 0)),
            pl.BlockSpec(b1.shape, lambda i: (0, 0)),
            pl.BlockSpec(w2.shape, lambda i: (0, 0)),
        ],
        out_specs=pl.BlockSpec((1, n_chunk, out_w), lambda i: (i, 0, 0)),
        out_shape=jax.ShapeDtypeStruct((b, n_chunk, out_w), jnp.float32),
        compiler_params=pltpu.CompilerParams(
            dimension_semantics=("parallel",), vmem_limit_bytes=VMEM_LIMIT),
        name="compress",
    )(x, pe, w1, b1, w2)


def _compress_params(k_pe, k_w1, k_b1, k_w2, v_pe, v_w1, v_b1, v_w2):
    hd, stride = HEAD_DIM, CMP_STRIDE
    n_str = 2 * NSA_KV_HEADS
    eye = jnp.eye(n_str, dtype=k_w1.dtype)

    def per_stream(k, v):
        return jnp.stack([k] * NSA_KV_HEADS + [v] * NSA_KV_HEADS)

    w1 = per_stream(k_w1, v_w1).reshape(n_str, 2, stride, hd, hd)
    w1 = jnp.einsum('jk,jhlde->hljdke', eye, w1).reshape(2, stride * n_str * hd, n_str * hd)
    pe = per_stream(k_pe, v_pe).reshape(n_str, 2, stride, hd)
    pe = jnp.transpose(pe, (1, 2, 0, 3)).reshape(2, 1, stride * n_str * hd)
    b1 = per_stream(k_b1, v_b1).reshape(1, n_str * hd)
    w2 = jnp.einsum('jk,jde->jdke', eye, per_stream(k_w2, v_w2)).reshape(n_str * hd, n_str * hd)
    return pe, w1.astype(jnp.bfloat16), b1, w2.astype(jnp.bfloat16)


def _nsa_kernel(q_ref, kvs_ref, kvw_ref, cmp_ref, gate_ref, ovlt_ref, out_ref, *, seq_len):
    tq = TQ_NSA
    g = NSA_GROUP
    heads = range(NSA_KV_HEADS)
    i = pl.program_id(1)
    t0 = i * tq
    n_cmp_pad = cmp_ref.shape[1]
    n_cmp = (seq_len - CMP_BLOCK) // CMP_STRIDE + 1
    n_slc = seq_len // SLC_BLOCK
    top_n = min(SLC_TOPK, n_slc)
    span = WINDOW + tq
    slc_shift = SLC_BLOCK.bit_length() - 1

    t_row = t0 + lax.broadcasted_iota(jnp.int32, (1, tq), 1)

    def k_lanes(hk):
        return slice(hk * HEAD_DIM, (hk + 1) * HEAD_DIM)

    def v_lanes(hk):
        return slice(KV_WIDTH + hk * HEAD_DIM, KV_WIDTH + (hk + 1) * HEAD_DIM)

    def tile_lanes(a):
        return jnp.concatenate([a] * g, axis=1)

    q_st = [jnp.concatenate(
        [q_ref[0, :, (hk * g + gi) * HEAD_DIM:(hk * g + gi + 1) * HEAD_DIM] for gi in range(g)], axis=0)
        for hk in heads]

    n_col = lax.broadcasted_iota(jnp.int32, (n_cmp_pad, 1), 0)
    cmask = tile_lanes((n_col * CMP_STRIDE + (CMP_BLOCK - 1) <= t_row) & (n_col < n_cmp))
    s_c = [_dot_nt(cmp_ref[0, :, k_lanes(hk)].astype(jnp.bfloat16), q_st[hk]) for hk in heads]
    p_c = []
    for hk in heads:
        m = jnp.max(jnp.where(cmask, s_c[hk], NEG_INF), axis=0, keepdims=True)
        e = jnp.where(cmask, jnp.exp2(s_c[hk] - m), 0.0)
        den = jnp.sum(e, axis=0, keepdims=True)
        p_c.append(e / jnp.where(den > 0.0, den, 1.0))
    o_cmp = [_dot_tn(cmp_ref[0, :, v_lanes(hk)].astype(jnp.bfloat16), p_c[hk].astype(jnp.bfloat16))
             for hk in heads]

    blk = lax.broadcasted_iota(jnp.int32, (n_slc, 1), 0)
    cur = jnp.right_shift(t_row, slc_shift)
    valid = blk * SLC_BLOCK <= t_row
    bonus = jnp.where((blk == 0) | (blk == cur) | (blk == cur - 1), FORCE_BONUS, 0.0)
    sel = []
    for hk in heads:
        p_sum = p_c[hk][:, 0:tq]
        for gi in range(1, g):
            p_sum = p_sum + p_c[hk][:, gi * tq:(gi + 1) * tq]
        p_slc = jnp.dot(ovlt_ref[...], p_sum, precision=lax.Precision.HIGHEST,
                        preferred_element_type=jnp.float32)
        score = jnp.where(valid, p_slc + bonus, -1.0)
        rank = jnp.zeros((n_slc, tq), jnp.float32)
        for c in range(n_slc):
            r = score[c:c + 1, :]
            rank = rank + ((r > score) | ((r >= score) & (blk > c))).astype(jnp.float32)
        sel.append(((rank < top_n) & (score >= 0.0)).astype(jnp.bfloat16))

    def slc_chunk(c, carry, causal):
        k0 = pl.multiple_of(c * TK_SLC, TK_SLC)
        tok = k0 + lax.broadcasted_iota(jnp.int32, (TK_SLC, 1), 0)
        expand = (jnp.right_shift(tok, slc_shift)
                  == lax.broadcasted_iota(jnp.int32, (TK_SLC, n_slc), 1)).astype(jnp.bfloat16)
        biases = []
        for hk in heads:
            keep = _dot(expand, sel[hk]) > 0.5
            if causal:
                keep = keep & (tok <= t_row)
            biases.append(tile_lanes(jnp.where(keep, 0.0, NEG_INF)))
        ss = [_dot_nt(kvs_ref[0, pl.ds(k0, TK_SLC), k_lanes(hk)], q_st[hk]) + biases[hk]
              for hk in heads]
        ps, stats = [], []
        for hk in heads:
            m_prev, l_prev, _ = carry[hk]
            m_new = jnp.maximum(m_prev, jnp.max(ss[hk], axis=0, keepdims=True))
            alpha = jnp.exp2(m_prev - m_new)
            p = jnp.exp2(ss[hk] - m_new)
            stats.append((m_new, alpha * l_prev + jnp.sum(p, axis=0, keepdims=True), alpha))
            ps.append(p.astype(jnp.bfloat16))
        new = []
        for hk in heads:
            m_new, l_new, alpha = stats[hk]
            pv = _dot_tn(kvs_ref[0, pl.ds(k0, TK_SLC), v_lanes(hk)], ps[hk])
            new.append((m_new, l_new, alpha * carry[hk][2] + pv))
        return tuple(new)

    last = (t0 + tq - 1) // TK_SLC
    init = tuple((jnp.full((1, g * tq), NEG_INF, jnp.float32),
                  jnp.zeros((1, g * tq), jnp.float32),
                  jnp.zeros((HEAD_DIM, g * tq), jnp.float32)) for _ in heads)
    carry = lax.fori_loop(0, last, lambda c, st: slc_chunk(c, st, False), init)
    carry = slc_chunk(last, carry, True)
    o_slc = [carry[hk][2] / carry[hk][1] for hk in heads]

    start = pl.multiple_of(jnp.maximum(t0 - WINDOW, 0), tq)
    diff = t_row - (start + lax.broadcasted_iota(jnp.int32, (span, 1), 0))
    wbias = tile_lanes(jnp.where((diff >= 0) & (diff < WINDOW), 0.0, NEG_INF))
    s_w = [_dot_nt(kvw_ref[0, pl.ds(start, span), k_lanes(hk)], q_st[hk]) + wbias for hk in heads]
    e_w = [jnp.exp2(s - jnp.max(s, axis=0, keepdims=True)) for s in s_w]
    o_win = [_dot_tn(kvw_ref[0, pl.ds(start, span), v_lanes(hk)], e_w[hk].astype(jnp.bfloat16))
             / jnp.sum(e_w[hk], axis=0, keepdims=True) for hk in heads]

    gates = gate_ref[0].T
    for hk in heads:
        for gi in range(g):
            hd = hk * g + gi
            cols = slice(gi * tq, (gi + 1) * tq)
            o = (gates[3 * hd:3 * hd + 1] * o_cmp[hk][:, cols]
                 + gates[3 * hd + 1:3 * hd + 2] * o_slc[hk][:, cols]
                 + gates[3 * hd + 2:3 * hd + 3] * o_win[hk][:, cols])
            out_ref[0, :, hd * HEAD_DIM:(hd + 1) * HEAD_DIM] = o.T.astype(out_ref.dtype)


def _nsa(proj, cmp_kv, gates, overlap_t):
    b, t, _ = proj.shape
    tq = TQ_NSA
    pair = 2 * KV_WIDTH
    return pl.pallas_call(
        functools.partial(_nsa_kernel, seq_len=t),
        grid=(b, t // tq),
        in_specs=[
            pl.BlockSpec((1, tq, NSA_WIDTH), lambda bi, i: (bi, i, COL_Q // NSA_WIDTH)),
            pl.BlockSpec((1, t, pair), lambda bi, i: (bi, 0, COL_KV // pair)),
            pl.BlockSpec((1, t, pair), lambda bi, i: (bi, 0, (COL_KV + pair) // pair)),
            pl.BlockSpec((1,) + cmp_kv.shape[1:], lambda bi, i: (bi, 0, 0)),
            pl.BlockSpec((1, tq, LANES), lambda bi, i: (bi, i, 0)),
            pl.BlockSpec(overlap_t.shape, lambda bi, i: (0, 0)),
        ],
        out_specs=pl.BlockSpec((1, tq, NSA_WIDTH), lambda bi, i: (bi, i, 0)),
        out_shape=jax.ShapeDtypeStruct((b, t, NSA_WIDTH), jnp.bfloat16),
        compiler_params=pltpu.CompilerParams(
            dimension_semantics=("parallel", "arbitrary"), vmem_limit_bytes=VMEM_LIMIT),
        name="nsa",
    )(proj, proj, proj, cmp_kv, gates, overlap_t)


def _sb_kernel(q_ref, k_ref, v_ref, out_ref):
    tq = TQ_SB
    i = pl.program_id(2)
    n_heads = q_ref.shape[2] // HEAD_DIM
    row = lax.broadcasted_iota(jnp.int32, (tq, tq), 0)
    col = lax.broadcasted_iota(jnp.int32, (tq, tq), 1)
    lower = (col > row).astype(jnp.bfloat16)
    strict = row < col
    lanes = [slice(h * HEAD_DIM, (h + 1) * HEAD_DIM) for h in range(n_heads)]
    qs = [q_ref[0, :, ln] for ln in lanes]

    def block(kb, state, diagonal):
        k0 = pl.multiple_of(kb * tq, tq)
        zs = [_dot_nt(k_ref[0, pl.ds(k0, tq), lanes[h]], qs[h]) for h in range(n_heads)]
        sps, logits = [], []
        for z in zs:
            neg_abs = lax.bitcast_convert_type(
                lax.bitcast_convert_type(z, jnp.uint32) | jnp.uint32(0x80000000), jnp.float32)
            sp = jnp.maximum(z, 0.0) + jnp.log(1.0 + jnp.exp(neg_abs))
            if diagonal:
                sp = jnp.where(strict, sp, 0.0)
            sps.append(sp)
            logits.append(z - sp)
        later = _dot(lower, jnp.concatenate([sp.astype(jnp.bfloat16) for sp in sps], axis=1))
        new_state = []
        for h in range(n_heads):
            carry, acc = state[h]
            later_h = later[:, h * tq:(h + 1) * tq]
            w = jnp.exp(logits[h] - later_h)
            if diagonal:
                w = jnp.where(strict, w, 0.0)
            pv = _dot_tn(v_ref[0, pl.ds(k0, tq), lanes[h]], w.astype(jnp.bfloat16))
            acc = pv if diagonal else acc + jnp.exp(-carry) * pv
            carry = carry + (later_h[0:1, :] + sps[h][0:1, :])
            new_state.append((carry, acc))
        return tuple(new_state)

    init = tuple((jnp.zeros((1, tq), jnp.float32), jnp.zeros((HEAD_DIM, tq), jnp.float32))
                 for _ in range(n_heads))
    state = block(i, init, True)
    state = lax.fori_loop(0, i, lambda it, st: block(i - 1 - it, st, False), state)
    for h in range(n_heads):
        out_ref[0, :, lanes[h]] = state[h][1].T.astype(out_ref.dtype)


def _sb(proj):
    b, t, _ = proj.shape
    tq = TQ_SB
    hw = SB_GROUP_WIDTH
    pairs = SB_WIDTH // hw
    q0, k0, v0 = COL_SB // hw, (COL_SB + SB_WIDTH) // hw, (COL_SB + 2 * SB_WIDTH) // hw
    return pl.pallas_call(
        _sb_kernel,
        grid=(b, pairs, t // tq),
        in_specs=[
            pl.BlockSpec((1, tq, hw), lambda bi, hp, i: (bi, i, q0 + hp)),
            pl.BlockSpec((1, t, hw), lambda bi, hp, i: (bi, 0, k0 + hp)),
            pl.BlockSpec((1, t, hw), lambda bi, hp, i: (bi, 0, v0 + hp)),
        ],
        out_specs=pl.BlockSpec((1, tq, hw), lambda bi, hp, i: (bi, i, hp)),
        out_shape=jax.ShapeDtypeStruct((b, t, SB_WIDTH), jnp.bfloat16),
        compiler_params=pltpu.CompilerParams(
            dimension_semantics=("parallel", "parallel", "arbitrary"), vmem_limit_bytes=VMEM_LIMIT),
        name="sb",
    )(proj, proj, proj)


def _mix_kernel(x_ref, on_ref, os_ref, gpre_ref, wg_ref, wpn_ref, wps_ref, wo_ref, gpost_ref, out_ref):
    x = x_ref[...]
    h = _rms_norm(x, gpre_ref[...]).astype(jnp.bfloat16)
    gm = jax.nn.sigmoid(_dot(h, wg_ref[...]))
    y = gm[:, :D_MODEL] * _dot(on_ref[...], wpn_ref[...]) + gm[:, D_MODEL:] * _dot(os_ref[...], wps_ref[...])
    m = _dot(y.astype(jnp.bfloat16), wo_ref[...])
    out_ref[...] = x + _rms_norm(m, gpost_ref[...])


def _mix(x2d, o_nsa, o_sb, g_pre, w_gm, w_pn, w_ps, w_out, g_post):
    n = x2d.shape[0]
    tm = TM_MIX
    const = lambda i: (0, 0)
    row = lambda i: (i, 0)
    return pl.pallas_call(
        _mix_kernel,
        grid=(n // tm,),
        in_specs=[
            pl.BlockSpec((tm, D_MODEL), row),
            pl.BlockSpec((tm, NSA_WIDTH), row),
            pl.BlockSpec((tm, SB_WIDTH), row),
            pl.BlockSpec((1, D_MODEL), const),
            pl.BlockSpec(w_gm.shape, const),
            pl.BlockSpec(w_pn.shape, const),
            pl.BlockSpec(w_ps.shape, const),
            pl.BlockSpec(w_out.shape, const),
            pl.BlockSpec((1, D_MODEL), const),
        ],
        out_specs=pl.BlockSpec((tm, D_MODEL), row),
        out_shape=jax.ShapeDtypeStruct((n, D_MODEL), jnp.float32),
        compiler_params=pltpu.CompilerParams(
            dimension_semantics=("parallel",), vmem_limit_bytes=VMEM_LIMIT),
        name="mix",
    )(x2d, o_nsa, o_sb, g_pre, w_gm, w_pn, w_ps, w_out, g_post)


def _ffn_kernel(x_ref, gpre_ref, wup_ref, cw_ref, cb_ref, wdn_ref, gpost_ref, out_ref, halo_ref):
    tm = TM_FFN
    ti = pl.program_id(1)

    @pl.when(ti == 0)
    def _():
        halo_ref[...] = jnp.zeros_like(halo_ref)

    x = x_ref[0]
    h = _rms_norm(x, gpre_ref[...]).astype(jnp.bfloat16)
    row = lax.broadcasted_iota(jnp.int32, (HALO, 1), 0)

    def conv(u, c0):
        cols = slice(c0, c0 + FF_CHUNK)
        prev = halo_ref[:, cols]
        halo_ref[:, cols] = u[tm - HALO:]
        r1 = pltpu.roll(u, 1, 0)
        r2 = pltpu.roll(u, 2, 0)
        head1 = jnp.where(row == 0, prev[HALO - 1:HALO], r1[:HALO])
        head2 = jnp.where(row == 0, prev[HALO - 2:HALO - 1], jnp.where(row == 1, prev[HALO - 1:HALO], r2[:HALO]))
        u1 = jnp.concatenate([head1, r1[HALO:]], axis=0)
        u2 = jnp.concatenate([head2, r2[HALO:]], axis=0)
        return cw_ref[0:1, cols] * u2 + cw_ref[1:2, cols] * u1 + cw_ref[2:3, cols] * u + cb_ref[:, cols]

    def up_dots(c):
        g0 = c * FF_CHUNK
        u0 = D_FF + c * FF_CHUNK
        return _dot(h, wup_ref[:, g0:g0 + FF_CHUNK]), _dot(h, wup_ref[:, u0:u0 + FF_CHUNK])

    n_chunks = D_FF // FF_CHUNK
    f = jnp.zeros((tm, D_MODEL), jnp.float32)
    ug, uu = up_dots(0)
    for c in range(n_chunks):
        nxt = up_dots(c + 1) if c + 1 < n_chunks else None
        g0 = c * FF_CHUNK
        gate = conv(ug, g0)
        up = conv(uu, D_FF + g0)
        act = (jax.nn.gelu(gate, approximate=True) * up).astype(jnp.bfloat16)
        f = f + _dot(act, wdn_ref[g0:g0 + FF_CHUNK, :])
        if nxt is not None:
            ug, uu = nxt
    out_ref[0] = x + _rms_norm(f, gpost_ref[...])


def _ffn(x, g_pre, w_up, conv_w, conv_b, w_down, g_post):
    b, t, _ = x.shape
    tm = TM_FFN
    const = lambda bi, i: (0, 0)
    return pl.pallas_call(
        _ffn_kernel,
        grid=(b, t // tm),
        in_specs=[
            pl.BlockSpec((1, tm, D_MODEL), lambda bi, i: (bi, i, 0)),
            pl.BlockSpec((1, D_MODEL), const),
            pl.BlockSpec(w_up.shape, const),
            pl.BlockSpec(conv_w.shape, const),
            pl.BlockSpec(conv_b.shape, const),
            pl.BlockSpec(w_down.shape, const),
            pl.BlockSpec((1, D_MODEL), const),
        ],
        out_specs=pl.BlockSpec((1, tm, D_MODEL), lambda bi, i: (bi, i, 0)),
        out_shape=jax.ShapeDtypeStruct(x.shape, jnp.float32),
        scratch_shapes=[pltpu.VMEM((HALO, 2 * D_FF), jnp.float32)],
        compiler_params=pltpu.CompilerParams(
            dimension_semantics=("parallel", "arbitrary"), vmem_limit_bytes=VMEM_LIMIT),
        name="ffn",
    )(x, g_pre, w_up, conv_w, conv_b, w_down, g_post)


def _overlap_matrix_t(seq_len, n_cols):
    n_cmp = (seq_len - CMP_BLOCK) // CMP_STRIDE + 1
    n_slc = seq_len // SLC_BLOCK
    ci = np.arange(n_cmp)[None, :] * CMP_STRIDE
    sj = np.arange(n_slc)[:, None] * SLC_BLOCK
    ovl = np.zeros((n_slc, n_cols), np.float32)
    ovl[:, :n_cmp] = ((ci < sj + SLC_BLOCK) & (sj < ci + CMP_BLOCK)).astype(np.float32)
    return jnp.asarray(ovl)


def kernel(x, positions, g_pre_mix, w_in, ck_pe, ck_w1, ck_b1, ck_w2, cv_pe, cv_w1, cv_b1, cv_w2,
           w_proj_nsa, w_proj_sb, w_out, g_post_mix, g_pre_ffn, w_up, conv_w, conv_b, w_down, g_post_ffn):
    b, t, d = x.shape
    depth = w_in.shape[0]
    assert d == D_MODEL and t % TM_FFN == 0 and t % TQ_SB == 0 and t >= WINDOW + TQ_NSA
    assert t % (CMP_STRIDE * 8) == 0 and t // SLC_BLOCK <= LANES
    bf = jnp.bfloat16
    n_chunk = t // CMP_STRIDE

    half = HEAD_DIM // 2
    inv_freq = ROPE_THETA ** (-jnp.arange(half, dtype=jnp.float32) / half)
    freq = jnp.tile(inv_freq, LANES // half)[None, :]
    sign = jnp.asarray(np.tile(np.concatenate([-np.ones(half), np.ones(half)]), LANES // HEAD_DIM),
                       jnp.float32)[None, :]
    overlap_t = _overlap_matrix_t(t, n_chunk)
    pos2d = positions.reshape(b * t, 1)

    c_gate = NSA_WIDTH + 6 * KV_WIDTH
    c_sb = c_gate + N_GATE
    c_merge = c_sb + 3 * SB_WIDTH

    for l in range(depth):
        wl = w_in[l]
        c_cmp = NSA_WIDTH + CMP_IN_WIDTH
        w_main = jnp.concatenate(
            [wl[:, c_sb:c_merge], wl[:, :NSA_WIDTH], wl[:, c_cmp:c_gate], wl[:, NSA_WIDTH:c_cmp],
             wl[:, c_gate:c_sb], jnp.zeros((d, LANES - N_GATE), wl.dtype)], axis=1).astype(bf)
        w_gm = wl[:, c_merge:].astype(bf)

        proj, cmp_in, gates = _in_proj(x.reshape(b * t, d), pos2d, g_pre_mix[l][None, :], w_main, freq, sign)
        proj = proj.reshape(b, t, PROJ_WIDTH)
        gates = gates.reshape(b, t, LANES)

        cmp_kv = _compress(
            cmp_in.reshape(b, n_chunk, CMP_STRIDE * CMP_IN_WIDTH),
            *_compress_params(ck_pe[l], ck_w1[l], ck_b1[l], ck_w2[l], cv_pe[l], cv_w1[l], cv_b1[l], cv_w2[l]))

        o_nsa = _nsa(proj, cmp_kv, gates, overlap_t)
        o_sb = _sb(proj)

        x = _mix(x.reshape(b * t, d), o_nsa.reshape(b * t, NSA_WIDTH), o_sb.reshape(b * t, SB_WIDTH),
                 g_pre_mix[l][None, :], w_gm, w_proj_nsa[l].astype(bf), w_proj_sb[l].astype(bf),
                 w_out[l].astype(bf), g_post_mix[l][None, :]).reshape(b, t, d)
        x = _ffn(x, g_pre_ffn[l][None, :], w_up[l].astype(bf), conv_w[l], conv_b[l][None, :],
                 w_down[l].astype(bf), g_post_ffn[l][None, :])
    return x
```

```python
import functools

import numpy as np
import jax
import jax.numpy as jnp
from jax import lax
from jax.experimental import pallas as pl
from jax.experimental.pallas import tpu as pltpu

D_MODEL = 1024
HEAD_DIM = 64
NSA_HEADS = 8
NSA_KV_HEADS = 2
NSA_GROUP = NSA_HEADS // NSA_KV_HEADS
SB_HEADS = 8
CMP_BLOCK = 32
CMP_STRIDE = 16
SLC_BLOCK = 64
SLC_TOPK = 16
WINDOW = 512
D_FF = 2816
ROPE_THETA = 10000.0
RMS_EPS = 1e-6
NEG_INF = -1e30
FORCE_BONUS = 1e4
LOG2_E = 1.4426950408889634
GELU_C0 = 0.7978845608028654
GELU_C1 = GELU_C0 * 0.044715

NSA_WIDTH = NSA_HEADS * HEAD_DIM
KV_WIDTH = NSA_KV_HEADS * HEAD_DIM
SB_WIDTH = SB_HEADS * HEAD_DIM
N_GATE = 3 * NSA_HEADS

LANES = 128
VMEM_LIMIT = 56 * 1024 * 1024

COL_SB = 0
COL_Q = COL_SB + 3 * SB_WIDTH
COL_KV = COL_Q + NSA_WIDTH
PROJ_WIDTH = COL_KV + 4 * KV_WIDTH
CMP_IN_WIDTH = 2 * KV_WIDTH
W_MAIN_WIDTH = PROJ_WIDTH + CMP_IN_WIDTH + LANES

TM_IN = 512
TQ_NSA = 256
TK_SLC = 256
TQ_SB = 256
SB_GROUP_WIDTH = 512
TM_MIX = 512
TM_FFN = 512
FF_CHUNK = 256
HALO = 8

assert COL_SB % SB_GROUP_WIDTH == 0 and SB_WIDTH % SB_GROUP_WIDTH == 0
assert COL_Q % NSA_WIDTH == 0 and COL_KV % (2 * KV_WIDTH) == 0


def _dot(a, b):
    return jnp.dot(a, b, preferred_element_type=jnp.float32)


def _dot_nt(a, b):
    return lax.dot_general(a, b, (((1,), (1,)), ((), ())), preferred_element_type=jnp.float32)


def _dot_tn(a, b):
    return lax.dot_general(a, b, (((0,), (0,)), ((), ())), preferred_element_type=jnp.float32)


def _rms_norm(xf, gain):
    y = xf * lax.rsqrt(jnp.mean(xf * xf, axis=-1, keepdims=True) + RMS_EPS)
    return y * gain


def _rope_blocks(y, cos, sin_signed, n_blocks, rope_block):
    lane = lax.broadcasted_iota(jnp.int32, (1, LANES), 1)
    first_half = (lane % HEAD_DIM) < (HEAD_DIM // 2)
    outs = []
    for b in range(n_blocks):
        blk = y[:, b * LANES:(b + 1) * LANES]
        if rope_block(b):
            partner = jnp.where(first_half,
                                pltpu.roll(blk, LANES - HEAD_DIM // 2, 1),
                                pltpu.roll(blk, HEAD_DIM // 2, 1))
            blk = blk * cos + partner * sin_signed
        outs.append(blk)
    return outs


def _in_proj_kernel(x_ref, pos_ref, g_ref, w_ref, freq_ref, sign_ref, proj_ref, cmp_ref, gate_ref):
    scale = HEAD_DIM ** -0.5
    h = _rms_norm(x_ref[...], g_ref[...]).astype(jnp.bfloat16)
    ang = pos_ref[...].astype(jnp.float32) * freq_ref[...]
    cos = jnp.cos(ang)
    sin_signed = jnp.sin(ang) * sign_ref[...]

    yq = _dot(h, w_ref[:, COL_Q:COL_Q + NSA_WIDTH])
    for b, blk in enumerate(_rope_blocks(yq, cos, sin_signed, NSA_WIDTH // LANES, lambda b: True)):
        proj_ref[:, COL_Q + b * LANES:COL_Q + (b + 1) * LANES] = (blk * (scale * LOG2_E)).astype(proj_ref.dtype)

    ykv = _dot(h, w_ref[:, COL_KV:COL_KV + 6 * KV_WIDTH])
    for b, blk in enumerate(_rope_blocks(ykv, cos, sin_signed, 6, lambda b: b % 2 == 0)):
        if b < 4:
            proj_ref[:, COL_KV + b * LANES:COL_KV + (b + 1) * LANES] = blk.astype(proj_ref.dtype)
        else:
            cmp_ref[:, (b - 4) * LANES:(b - 3) * LANES] = blk.astype(cmp_ref.dtype)

    ysb = _dot(h, w_ref[:, COL_SB:COL_SB + 3 * SB_WIDTH])
    proj_ref[:, COL_SB:COL_SB + SB_WIDTH] = (ysb[:, :SB_WIDTH] * scale).astype(proj_ref.dtype)
    proj_ref[:, COL_SB + SB_WIDTH:COL_SB + 3 * SB_WIDTH] = ysb[:, SB_WIDTH:].astype(proj_ref.dtype)

    gate_ref[...] = jax.nn.sigmoid(_dot(h, w_ref[:, W_MAIN_WIDTH - LANES:W_MAIN_WIDTH]))


def _in_proj(x2d, pos2d, gain, w_main, freq, sign):
    n = x2d.shape[0]
    tm = TM_IN
    const = lambda i: (0, 0)
    return pl.pallas_call(
        _in_proj_kernel,
        grid=(n // tm,),
        in_specs=[
            pl.BlockSpec((tm, D_MODEL), lambda i: (i, 0)),
            pl.BlockSpec((tm, 1), lambda i: (i, 0)),
            pl.BlockSpec((1, D_MODEL), const),
            pl.BlockSpec((D_MODEL, W_MAIN_WIDTH), const),
            pl.BlockSpec((1, LANES), const),
            pl.BlockSpec((1, LANES), const),
        ],
        out_specs=[
            pl.BlockSpec((tm, PROJ_WIDTH), lambda i: (i, 0)),
            pl.BlockSpec((tm, CMP_IN_WIDTH), lambda i: (i, 0)),
            pl.BlockSpec((tm, LANES), lambda i: (i, 0)),
        ],
        out_shape=[
            jax.ShapeDtypeStruct((n, PROJ_WIDTH), jnp.bfloat16),
            jax.ShapeDtypeStruct((n, CMP_IN_WIDTH), jnp.bfloat16),
            jax.ShapeDtypeStruct((n, LANES), jnp.float32),
        ],
        compiler_params=pltpu.CompilerParams(
            dimension_semantics=("parallel",), vmem_limit_bytes=VMEM_LIMIT),
        name="in_proj",
    )(x2d, pos2d, gain, w_main, freq, sign)


def _compress_kernel(x_ref, pe_ref, w1_ref, b1_ref, w2_ref, out_ref):
    x = x_ref[0]
    n_chunk = x.shape[0]
    top = _dot(x, w1_ref[0])
    bot = _dot(x, w1_ref[1])
    bot_next = pltpu.roll(bot, n_chunk - 1, 0)
    pe = jnp.broadcast_to(pe_ref[...], (2, 8, x.shape[1])).astype(jnp.bfloat16)
    bias = _dot(pe[0], w1_ref[0])[0:1] + _dot(pe[1], w1_ref[1])[0:1] + b1_ref[...]
    hid = jax.nn.gelu(top + bot_next + bias, approximate=True)
    out_ref[0] = _dot(hid.astype(jnp.bfloat16), w2_ref[...])


def _compress(x, pe, w1, b1, w2):
    b, n_chunk, width = x.shape
    out_w = w2.shape[1]
    return pl.pallas_call(
        _compress_kernel,
        grid=(b,),
        in_specs=[
            pl.BlockSpec((1, n_chunk, width), lambda i: (i, 0, 0)),
            pl.BlockSpec(pe.shape, lambda i: (0, 0, 0)),
            pl.BlockSpec(w1.shape, lambda i: (0, 0, 0)),
            pl.BlockSpec(b1.shape, lambda i: (0, 0)),
            pl.BlockSpec(w2.shape, lambda i: (0, 0)),
        ],
        out_specs=pl.BlockSpec((1, n_chunk, out_w), lambda i: (i, 0, 0)),
        out_shape=jax.ShapeDtypeStruct((b, n_chunk, out_w), jnp.float32),
        compiler_params=pltpu.CompilerParams(
            dimension_semantics=("parallel",), vmem_limit_bytes=VMEM_LIMIT),
        name="compress",
    )(x, pe, w1, b1, w2)


def _compress_params(k_pe, k_w1, k_b1, k_w2, v_pe, v_w1, v_b1, v_w2):
    hd, stride = HEAD_DIM, CMP_STRIDE
    n_str = 2 * NSA_KV_HEADS
    eye = jnp.eye(n_str, dtype=k_w1.dtype)

    def per_stream(k, v):
        return jnp.stack([k] * NSA_KV_HEADS + [v] * NSA_KV_HEADS)

    w1 = per_stream(k_w1, v_w1).reshape(n_str, 2, stride, hd, hd)
    w1 = jnp.einsum('jk,jhlde->hljdke', eye, w1).reshape(2, stride * n_str * hd, n_str * hd)
    pe = per_stream(k_pe, v_pe).reshape(n_str, 2, stride, hd)
    pe = jnp.transpose(pe, (1, 2, 0, 3)).reshape(2, 1, stride * n_str * hd)
    b1 = per_stream(k_b1, v_b1).reshape(1, n_str * hd)
    w2 = jnp.einsum('jk,jde->jdke', eye, per_stream(k_w2, v_w2)).reshape(n_str * hd, n_str * hd)
    return pe, w1.astype(jnp.bfloat16), b1, w2.astype(jnp.bfloat16)


def _nsa_kernel(q_ref, kvs_ref, kvw_ref, cmp_ref, gate_ref, ovlt_ref, out_ref, *, seq_len):
    tq = TQ_NSA
    g = NSA_GROUP
    heads = range(NSA_KV_HEADS)
    i = pl.program_id(1)
    t0 = i * tq
    n_cmp_pad = cmp_ref.shape[1]
    n_cmp = (seq_len - CMP_BLOCK) // CMP_STRIDE + 1
    n_slc = seq_len // SLC_BLOCK
    top_n = min(SLC_TOPK, n_slc)
    span = WINDOW + tq
    slc_shift = SLC_BLOCK.bit_length() - 1

    t_row = t0 + lax.broadcasted_iota(jnp.int32, (1, tq), 1)

    def k_lanes(hk):
        return slice(hk * HEAD_DIM, (hk + 1) * HEAD_DIM)

    def v_lanes(hk):
        return slice(KV_WIDTH + hk * HEAD_DIM, KV_WIDTH + (hk + 1) * HEAD_DIM)

    def tile_lanes(a):
        return jnp.concatenate([a] * g, axis=1)

    q_st = [jnp.concatenate(
        [q_ref[0, :, (hk * g + gi) * HEAD_DIM:(hk * g + gi + 1) * HEAD_DIM] for gi in range(g)], axis=0)
        for hk in heads]

    n_col = lax.broadcasted_iota(jnp.int32, (n_cmp_pad, 1), 0)
    cmask = tile_lanes((n_col * CMP_STRIDE + (CMP_BLOCK - 1) <= t_row) & (n_col < n_cmp))
    s_c = [_dot_nt(cmp_ref[0, :, k_lanes(hk)].astype(jnp.bfloat16), q_st[hk]) for hk in heads]
    p_c = []
    for hk in heads:
        m = jnp.max(jnp.where(cmask, s_c[hk], NEG_INF), axis=0, keepdims=True)
        e = jnp.where(cmask, jnp.exp2(s_c[hk] - m), 0.0)
        den = jnp.sum(e, axis=0, keepdims=True)
        p_c.append(e / jnp.where(den > 0.0, den, 1.0))
    o_cmp = [_dot_tn(cmp_ref[0, :, v_lanes(hk)].astype(jnp.bfloat16), p_c[hk].astype(jnp.bfloat16))
             for hk in heads]

    blk = lax.broadcasted_iota(jnp.int32, (n_slc, 1), 0)
    cur = jnp.right_shift(t_row, slc_shift)
    valid = blk * SLC_BLOCK <= t_row
    bonus = jnp.where((blk == 0) | (blk == cur) | (blk == cur - 1), FORCE_BONUS, 0.0)
    sel = []
    for hk in heads:
        p_sum = p_c[hk][:, 0:tq]
        for gi in range(1, g):
            p_sum = p_sum + p_c[hk][:, gi * tq:(gi + 1) * tq]
        p_slc = jnp.dot(ovlt_ref[...], p_sum, precision=lax.Precision.HIGHEST,
                        preferred_element_type=jnp.float32)
        score = jnp.where(valid, p_slc + bonus, -1.0)
        rank = jnp.zeros((n_slc, tq), jnp.float32)
        for c in range(n_slc):
            r = score[c:c + 1, :]
            rank = rank + ((r > score) | ((r >= score) & (blk > c))).astype(jnp.float32)
        sel.append(((rank < top_n) & (score >= 0.0)).astype(jnp.bfloat16))

    def slc_chunk(c, carry, causal):
        k0 = pl.multiple_of(c * TK_SLC, TK_SLC)
        tok = k0 + lax.broadcasted_iota(jnp.int32, (TK_SLC, 1), 0)
        expand = (jnp.right_shift(tok, slc_shift)
                  == lax.broadcasted_iota(jnp.int32, (TK_SLC, n_slc), 1)).astype(jnp.bfloat16)
        biases = []
        for hk in heads:
            keep = _dot(expand, sel[hk]) > 0.5
            if causal:
                keep = keep & (tok <= t_row)
            biases.append(tile_lanes(jnp.where(keep, 0.0, NEG_INF)))
        ss = [_dot_nt(kvs_ref[0, pl.ds(k0, TK_SLC), k_lanes(hk)], q_st[hk]) + biases[hk]
              for hk in heads]
        ps, stats = [], []
        for hk in heads:
            m_prev, l_prev, _ = carry[hk]
            m_new = jnp.maximum(m_prev, jnp.max(ss[hk], axis=0, keepdims=True))
            alpha = jnp.exp2(m_prev - m_new)
            p = jnp.exp2(ss[hk] - m_new)
            stats.append((m_new, alpha * l_prev + jnp.sum(p, axis=0, keepdims=True), alpha))
            ps.append(p.astype(jnp.bfloat16))
        new = []
        for hk in heads:
            m_new, l_new, alpha = stats[hk]
            pv = _dot_tn(kvs_ref[0, pl.ds(k0, TK_SLC), v_lanes(hk)], ps[hk])
            new.append((m_new, l_new, alpha * carry[hk][2] + pv))
        return tuple(new)

    last = (t0 + tq - 1) // TK_SLC
    init = tuple((jnp.full((1, g * tq), NEG_INF, jnp.float32),
                  jnp.zeros((1, g * tq), jnp.float32),
                  jnp.zeros((HEAD_DIM, g * tq), jnp.float32)) for _ in heads)
    carry = lax.fori_loop(0, last, lambda c, st: slc_chunk(c, st, False), init)
    carry = slc_chunk(last, carry, True)
    o_slc = [carry[hk][2] / carry[hk][1] for hk in heads]

    start = pl.multiple_of(jnp.maximum(t0 - WINDOW, 0), tq)
    diff = t_row - (start + lax.broadcasted_iota(jnp.int32, (span, 1), 0))
    wbias = tile_lanes(jnp.where((diff >= 0) & (diff < WINDOW), 0.0, NEG_INF))
    s_w = [_dot_nt(kvw_ref[0, pl.ds(start, span), k_lanes(hk)], q_st[hk]) + wbias for hk in heads]
    e_w = [jnp.exp2(s - jnp.max(s, axis=0, keepdims=True)) for s in s_w]
    o_win = [_dot_tn(kvw_ref[0, pl.ds(start, span), v_lanes(hk)], e_w[hk].astype(jnp.bfloat16))
             / jnp.sum(e_w[hk], axis=0, keepdims=True) for hk in heads]

    gates = gate_ref[0].T
    for hk in heads:
        for gi in range(g):
            hd = hk * g + gi
            cols = slice(gi * tq, (gi + 1) * tq)
            o = (gates[3 * hd:3 * hd + 1] * o_cmp[hk][:, cols]
                 + gates[3 * hd + 1:3 * hd + 2] * o_slc[hk][:, cols]
                 + gates[3 * hd + 2:3 * hd + 3] * o_win[hk][:, cols])
            out_ref[0, :, hd * HEAD_DIM:(hd + 1) * HEAD_DIM] = o.T.astype(out_ref.dtype)


def _nsa(proj, cmp_kv, gates, overlap_t):
    b, t, _ = proj.shape
    tq = TQ_NSA
    pair = 2 * KV_WIDTH
    return pl.pallas_call(
        functools.partial(_nsa_kernel, seq_len=t),
        grid=(b, t // tq),
        in_specs=[
            pl.BlockSpec((1, tq, NSA_WIDTH), lambda bi, i: (bi, i, COL_Q // NSA_WIDTH)),
            pl.BlockSpec((1, t, pair), lambda bi, i: (bi, 0, COL_KV // pair)),
            pl.BlockSpec((1, t, pair), lambda bi, i: (bi, 0, (COL_KV + pair) // pair)),
            pl.BlockSpec((1,) + cmp_kv.shape[1:], lambda bi, i: (bi, 0, 0)),
            pl.BlockSpec((1, tq, LANES), lambda bi, i: (bi, i, 0)),
            pl.BlockSpec(overlap_t.shape, lambda bi, i: (0, 0)),
        ],
        out_specs=pl.BlockSpec((1, tq, NSA_WIDTH), lambda bi, i: (bi, i, 0)),
        out_shape=jax.ShapeDtypeStruct((b, t, NSA_WIDTH), jnp.bfloat16),
        compiler_params=pltpu.CompilerParams(
            dimension_semantics=("parallel", "arbitrary"), vmem_limit_bytes=VMEM_LIMIT),
        name="nsa",
    )(proj, proj, proj, cmp_kv, gates, overlap_t)


def _sb_kernel(q_ref, k_ref, v_ref, out_ref):
    tq = TQ_SB
    i = pl.program_id(2)
    n_heads = q_ref.shape[2] // HEAD_DIM
    row = lax.broadcasted_iota(jnp.int32, (tq, tq), 0)
    col = lax.broadcasted_iota(jnp.int32, (tq, tq), 1)
    lower = (col > row).astype(jnp.bfloat16)
    strict = row < col
    lanes = [slice(h * HEAD_DIM, (h + 1) * HEAD_DIM) for h in range(n_heads)]
    qs = [q_ref[0, :, ln] for ln in lanes]

    def block(kb, state, diagonal):
        k0 = pl.multiple_of(kb * tq, tq)
        zs = [_dot_nt(k_ref[0, pl.ds(k0, tq), lanes[h]], qs[h]) for h in range(n_heads)]
        sps, logits = [], []
        for z in zs:
            neg_abs = lax.bitcast_convert_type(
                lax.bitcast_convert_type(z, jnp.uint32) | jnp.uint32(0x80000000), jnp.float32)
            sp = jnp.maximum(z, 0.0) + jnp.log(1.0 + jnp.exp(neg_abs))
            if diagonal:
                sp = jnp.where(strict, sp, 0.0)
            sps.append(sp)
            logits.append(z - sp)
        later = _dot(lower, jnp.concatenate([sp.astype(jnp.bfloat16) for sp in sps], axis=1))
        new_state = []
        for h in range(n_heads):
            carry, acc = state[h]
            later_h = later[:, h * tq:(h + 1) * tq]
            w = jnp.exp(logits[h] - later_h)
            if diagonal:
                w = jnp.where(strict, w, 0.0)
            pv = _dot_tn(v_ref[0, pl.ds(k0, tq), lanes[h]], w.astype(jnp.bfloat16))
            acc = pv if diagonal else acc + jnp.exp(-carry) * pv
            carry = carry + (later_h[0:1, :] + sps[h][0:1, :])
            new_state.append((carry, acc))
        return tuple(new_state)

    init = tuple((jnp.zeros((1, tq), jnp.float32), jnp.zeros((HEAD_DIM, tq), jnp.float32))
                 for _ in range(n_heads))
    state = block(i, init, True)
    state = lax.fori_loop(0, i, lambda it, st: block(i - 1 - it, st, False), state)
    for h in range(n_heads):
        out_ref[0, :, lanes[h]] = state[h][1].T.astype(out_ref.dtype)


def _sb(proj):
    b, t, _ = proj.shape
    tq = TQ_SB
    hw = SB_GROUP_WIDTH
    pairs = SB_WIDTH // hw
    q0, k0, v0 = COL_SB // hw, (COL_SB + SB_WIDTH) // hw, (COL_SB + 2 * SB_WIDTH) // hw
    return pl.pallas_call(
        _sb_kernel,
        grid=(b, pairs, t // tq),
        in_specs=[
            pl.BlockSpec((1, tq, hw), lambda bi, hp, i: (bi, i, q0 + hp)),
            pl.BlockSpec((1, t, hw), lambda bi, hp, i: (bi, 0, k0 + hp)),
            pl.BlockSpec((1, t, hw), lambda bi, hp, i: (bi, 0, v0 + hp)),
        ],
        out_specs=pl.BlockSpec((1, tq, hw), lambda bi, hp, i: (bi, i, hp)),
        out_shape=jax.ShapeDtypeStruct((b, t, SB_WIDTH), jnp.bfloat16),
        compiler_params=pltpu.CompilerParams(
            dimension_semantics=("parallel", "parallel", "arbitrary"), vmem_limit_bytes=VMEM_LIMIT),
        name="sb",
    )(proj, proj, proj)


def _mix_kernel(x_ref, on_ref, os_ref, gpre_ref, wg_ref, wpn_ref, wps_ref, wo_ref, gpost_ref, out_ref):
    x = x_ref[...]
    h = _rms_norm(x, gpre_ref[...]).astype(jnp.bfloat16)
    gm = jax.nn.sigmoid(_dot(h, wg_ref[...]))
    y = gm[:, :D_MODEL] * _dot(on_ref[...], wpn_ref[...]) + gm[:, D_MODEL:] * _dot(os_ref[...], wps_ref[...])
    m = _dot(y.astype(jnp.bfloat16), wo_ref[...])
    out_ref[...] = x + _rms_norm(m, gpost_ref[...])


def _mix(x2d, o_nsa, o_sb, g_pre, w_gm, w_pn, w_ps, w_out, g_post):
    n = x2d.shape[0]
    tm = TM_MIX
    const = lambda i: (0, 0)
    row = lambda i: (i, 0)
    return pl.pallas_call(
        _mix_kernel,
        grid=(n // tm,),
        in_specs=[
            pl.BlockSpec((tm, D_MODEL), row),
            pl.BlockSpec((tm, NSA_WIDTH), row),
            pl.BlockSpec((tm, SB_WIDTH), row),
            pl.BlockSpec((1, D_MODEL), const),
            pl.BlockSpec(w_gm.shape, const),
            pl.BlockSpec(w_pn.shape, const),
            pl.BlockSpec(w_ps.shape, const),
            pl.BlockSpec(w_out.shape, const),
            pl.BlockSpec((1, D_MODEL), const),
        ],
        out_specs=pl.BlockSpec((tm, D_MODEL), row),
        out_shape=jax.ShapeDtypeStruct((n, D_MODEL), jnp.float32),
        compiler_params=pltpu.CompilerParams(
            dimension_semantics=("parallel",), vmem_limit_bytes=VMEM_LIMIT),
        name="mix",
    )(x2d, o_nsa, o_sb, g_pre, w_gm, w_pn, w_ps, w_out, g_post)


def _ffn_kernel(x_ref, gpre_ref, wup_ref, cw_ref, cb_ref, wdn_ref, gpost_ref, out_ref, halo_ref):
    tm = TM_FFN
    ti = pl.program_id(1)

    @pl.when(ti == 0)
    def _():
        halo_ref[...] = jnp.zeros_like(halo_ref)

    x = x_ref[0]
    h = _rms_norm(x, gpre_ref[...]).astype(jnp.bfloat16)
    row = lax.broadcasted_iota(jnp.int32, (HALO, 1), 0)

    def conv(u, c0):
        cols = slice(c0, c0 + FF_CHUNK)
        prev = halo_ref[:, cols]
        halo_ref[:, cols] = u[tm - HALO:]
        r1 = pltpu.roll(u, 1, 0)
        r2 = pltpu.roll(u, 2, 0)
        head1 = jnp.where(row == 0, prev[HALO - 1:HALO], r1[:HALO])
        head2 = jnp.where(row == 0, prev[HALO - 2:HALO - 1], jnp.where(row == 1, prev[HALO - 1:HALO], r2[:HALO]))
        u1 = jnp.concatenate([head1, r1[HALO:]], axis=0)
        u2 = jnp.concatenate([head2, r2[HALO:]], axis=0)
        return cw_ref[0:1, cols] * u2 + cw_ref[1:2, cols] * u1 + cw_ref[2:3, cols] * u + cb_ref[:, cols]

    def up_dots(c):
        g0 = c * FF_CHUNK
        u0 = D_FF + c * FF_CHUNK
        return _dot(h, wup_ref[:, g0:g0 + FF_CHUNK]), _dot(h, wup_ref[:, u0:u0 + FF_CHUNK])

    n_chunks = D_FF // FF_CHUNK
    f = jnp.zeros((tm, D_MODEL), jnp.float32)
    ug, uu = up_dots(0)
    for c in range(n_chunks):
        nxt = up_dots(c + 1) if c + 1 < n_chunks else None
        g0 = c * FF_CHUNK
        gate = conv(ug, g0)
        half_up = conv(uu, D_FF + g0)
        t = jnp.tanh(gate * (GELU_C0 + GELU_C1 * (gate * gate)))
        hx = gate * half_up
        act = (hx + hx * t).astype(jnp.bfloat16)
        f = f + _dot(act, wdn_ref[g0:g0 + FF_CHUNK, :])
        if nxt is not None:
            ug, uu = nxt
    out_ref[0] = x + _rms_norm(f, gpost_ref[...])


def _ffn(x, g_pre, w_up, conv_w, conv_b, w_down, g_post):
    b, t, _ = x.shape
    tm = TM_FFN
    const = lambda bi, i: (0, 0)
    return pl.pallas_call(
        _ffn_kernel,
        grid=(b, t // tm),
        in_specs=[
            pl.BlockSpec((1, tm, D_MODEL), lambda bi, i: (bi, i, 0)),
            pl.BlockSpec((1, D_MODEL), const),
            pl.BlockSpec(w_up.shape, const),
            pl.BlockSpec(conv_w.shape, const),
            pl.BlockSpec(conv_b.shape, const),
            pl.BlockSpec(w_down.shape, const),
            pl.BlockSpec((1, D_MODEL), const),
        ],
        out_specs=pl.BlockSpec((1, tm, D_MODEL), lambda bi, i: (bi, i, 0)),
        out_shape=jax.ShapeDtypeStruct(x.shape, jnp.float32),
        scratch_shapes=[pltpu.VMEM((HALO, 2 * D_FF), jnp.float32)],
        compiler_params=pltpu.CompilerParams(
            dimension_semantics=("parallel", "arbitrary"), vmem_limit_bytes=VMEM_LIMIT),
        name="ffn",
    )(x, g_pre, w_up, conv_w, conv_b, w_down, g_post)


def _overlap_matrix_t(seq_len, n_cols):
    n_cmp = (seq_len - CMP_BLOCK) // CMP_STRIDE + 1
    n_slc = seq_len // SLC_BLOCK
    ci = np.arange(n_cmp)[None, :] * CMP_STRIDE
    sj = np.arange(n_slc)[:, None] * SLC_BLOCK
    ovl = np.zeros((n_slc, n_cols), np.float32)
    ovl[:, :n_cmp] = ((ci < sj + SLC_BLOCK) & (sj < ci + CMP_BLOCK)).astype(np.float32)
    return jnp.asarray(ovl)


def kernel(x, positions, g_pre_mix, w_in, ck_pe, ck_w1, ck_b1, ck_w2, cv_pe, cv_w1, cv_b1, cv_w2,
           w_proj_nsa, w_proj_sb, w_out, g_post_mix, g_pre_ffn, w_up, conv_w, conv_b, w_down, g_post_ffn):
    b, t, d = x.shape
    depth = w_in.shape[0]
    assert d == D_MODEL and t % TM_FFN == 0 and t % TQ_SB == 0 and t >= WINDOW + TQ_NSA
    assert t % (CMP_STRIDE * 8) == 0 and t // SLC_BLOCK <= LANES
    bf = jnp.bfloat16
    n_chunk = t // CMP_STRIDE

    half = HEAD_DIM // 2
    inv_freq = ROPE_THETA ** (-jnp.arange(half, dtype=jnp.float32) / half)
    freq = jnp.tile(inv_freq, LANES // half)[None, :]
    sign = jnp.asarray(np.tile(np.concatenate([-np.ones(half), np.ones(half)]), LANES // HEAD_DIM),
                       jnp.float32)[None, :]
    overlap_t = _overlap_matrix_t(t, n_chunk)
    pos2d = positions.reshape(b * t, 1)

    c_gate = NSA_WIDTH + 6 * KV_WIDTH
    c_sb = c_gate + N_GATE
    c_merge = c_sb + 3 * SB_WIDTH

    for l in range(depth):
        wl = w_in[l]
        c_cmp = NSA_WIDTH + CMP_IN_WIDTH
        w_main = jnp.concatenate(
            [wl[:, c_sb:c_merge], wl[:, :NSA_WIDTH], wl[:, c_cmp:c_gate], wl[:, NSA_WIDTH:c_cmp],
             wl[:, c_gate:c_sb], jnp.zeros((d, LANES - N_GATE), wl.dtype)], axis=1).astype(bf)
        w_gm = wl[:, c_merge:].astype(bf)

        proj, cmp_in, gates = _in_proj(x.reshape(b * t, d), pos2d, g_pre_mix[l][None, :], w_main, freq, sign)
        proj = proj.reshape(b, t, PROJ_WIDTH)
        gates = gates.reshape(b, t, LANES)

        cmp_kv = _compress(
            cmp_in.reshape(b, n_chunk, CMP_STRIDE * CMP_IN_WIDTH),
            *_compress_params(ck_pe[l], ck_w1[l], ck_b1[l], ck_w2[l], cv_pe[l], cv_w1[l], cv_b1[l], cv_w2[l]))

        o_nsa = _nsa(proj, cmp_kv, gates, overlap_t)
        o_sb = _sb(proj)

        x = _mix(x.reshape(b * t, d), o_nsa.reshape(b * t, NSA_WIDTH), o_sb.reshape(b * t, SB_WIDTH),
                 g_pre_mix[l][None, :], w_gm, w_proj_nsa[l].astype(bf), w_proj_sb[l].astype(bf),
                 w_out[l].astype(bf), g_post_mix[l][None, :]).reshape(b, t, d)
        up_scale = jnp.concatenate([jnp.ones((D_FF,), jnp.float32), jnp.full((D_FF,), 0.5, jnp.float32)])
        x = _ffn(x, g_pre_ffn[l][None, :], w_up[l].astype(bf), conv_w[l] * up_scale, (conv_b[l] * up_scale)[None, :],
                 w_down[l].astype(bf), g_post_ffn[l][None, :])
    return x
```
